```python
import jax, jax.numpy as jnp
from jax import lax
import numpy as np

D_MODEL = 1024
BATCH = 2
SEQ = 16384
DEPTH = 4

GRID_W = 64
Q_BLOCK = 128
ROPE_THETA = 10000.0
EPS = 1e-6
D_FF = 2816
GQA_HEADS = 8
GQA_KV_HEADS = 2
GQA_HEAD_DIM = 64
SC_WIDTH = 512
SC_KERNEL = 3
MLA_HEADS = 8
MLA_Q_LORA = 384
MLA_KV_LORA = 256
MLA_NOPE = 64
MLA_ROPE = 32
MLA_V = 64
LRU_WIDTH = 512
LRU_BLOCKS = 8
LRU_BLOCK_W = LRU_WIDTH // LRU_BLOCKS
LRU_CONV = 4
LRU_C = 8.0
N_BRANCH = 4
BRANCH_W = 512
IN_SIZES = (
    GQA_HEADS * GQA_HEAD_DIM,
    GQA_KV_HEADS * GQA_HEAD_DIM,
    GQA_KV_HEADS * GQA_HEAD_DIM,
    3 * SC_WIDTH,
    MLA_Q_LORA,
    MLA_KV_LORA,
    MLA_ROPE,
    2 * LRU_WIDTH,
    N_BRANCH * D_MODEL,
)
IN_TOTAL = sum(IN_SIZES)

kernel_name = "hybrid_gqa_shortconv_mla_rglru_macaron_encoder"


def rmsnorm(x, g):
    xf = x.astype(jnp.float32)
    y = xf * lax.rsqrt(jnp.mean(xf * xf, axis=-1, keepdims=True) + EPS)
    return (y * g.astype(jnp.float32)).astype(x.dtype)


def swiglu_ffn(x, g, wi, wo):
    gate, up = jnp.split(rmsnorm(x, g) @ wi, 2, axis=-1)
    return (jax.nn.silu(gate) * up) @ wo


def rope_angles(pos, dim):
    inv = ROPE_THETA ** (-jnp.arange(0, dim, 2, dtype=jnp.float32) / dim)
    return pos[:, None] * inv[None, :]


def apply_rope_1d(x, ang):
    cos = jnp.cos(ang)[None, :, None, :].astype(x.dtype)
    sin = jnp.sin(ang)[None, :, None, :].astype(x.dtype)
    x1, x2 = jnp.split(x, 2, axis=-1)
    return jnp.concatenate([x1 * cos - x2 * sin, x2 * cos + x1 * sin], axis=-1)


def apply_axial_rope(x, ang_row, ang_col):
    half = x.shape[-1] // 2
    return jnp.concatenate([apply_rope_1d(x[..., :half], ang_row),
                            apply_rope_1d(x[..., half:], ang_col)], axis=-1)


def block_attention(q, k, v):
    b, hk, g, s, d = q.shape
    nb = s // Q_BLOCK
    qb = jnp.moveaxis(q.reshape(b, hk, g, nb, Q_BLOCK, d), 3, 0)
    scale = d ** -0.5

    def one_block(qi):
        sc = jnp.einsum('bkgqd,bksd->bkgqs', qi, k).astype(jnp.float32) * scale
        p = jax.nn.softmax(sc, axis=-1).astype(v.dtype)
        return jnp.einsum('bkgqs,bksv->bkgqv', p, v)

    o = lax.map(one_block, qb)
    return jnp.moveaxis(o, 0, 3).reshape(b, hk, g, s, v.shape[-1])


def dwconv(x, w, bias, left):
    width = w.shape[0]
    s = x.shape[1]
    xp = jnp.pad(x, ((0, 0), (left, width - 1 - left), (0, 0)))
    out = bias
    for tap in range(width):
        out = out + w[tap] * xp[:, tap:tap + s]
    return out


def gqa_mixer(q, k, v, qn, kn, ang_r, ang_c):
    b, s, _ = q.shape
    grp = GQA_HEADS // GQA_KV_HEADS
    q = q.reshape(b, s, GQA_HEADS, GQA_HEAD_DIM)
    k = k.reshape(b, s, GQA_KV_HEADS, GQA_HEAD_DIM)
    v = v.reshape(b, s, GQA_KV_HEADS, GQA_HEAD_DIM)
    q = apply_axial_rope(rmsnorm(q, qn), ang_r, ang_c)
    k = apply_axial_rope(rmsnorm(k, kn), ang_r, ang_c)
    qh = q.reshape(b, s, GQA_KV_HEADS, grp, GQA_HEAD_DIM).transpose(0, 2, 3, 1, 4)
    o = block_attention(qh, k.transpose(0, 2, 1, 3), v.transpose(0, 2, 1, 3))
    return o.transpose(0, 3, 1, 2, 4).reshape(b, s, GQA_HEADS * GQA_HEAD_DIM)


def shortconv_mixer(u, w, bias):
    bg, cg, xs = jnp.split(u, 3, axis=-1)
    return bg * dwconv(cg * xs, w, bias, left=SC_KERNEL // 2)


def mla_mixer(q_lat, kv_lat, k_rope, qa_norm, wq_up, kva_norm, wkv_up, qn, kn, ang_r, ang_c):
    b, s, _ = q_lat.shape
    q = (rmsnorm(q_lat, qa_norm) @ wq_up).reshape(b, s, MLA_HEADS, MLA_NOPE + MLA_ROPE)
    kv = (rmsnorm(kv_lat, kva_norm) @ wkv_up).reshape(b, s, MLA_HEADS, MLA_NOPE + MLA_V)
    k_nope, v = kv[..., :MLA_NOPE], kv[..., MLA_NOPE:]
    k_r = jnp.broadcast_to(k_rope[:, :, None, :], (b, s, MLA_HEADS, MLA_ROPE))
    k = jnp.concatenate([k_nope, k_r], axis=-1)
    q = rmsnorm(q, qn)
    k = rmsnorm(k, kn)
    q = jnp.concatenate([q[..., :MLA_NOPE], apply_axial_rope(q[..., MLA_NOPE:], ang_r, ang_c)], axis=-1)
    k = jnp.concatenate([k[..., :MLA_NOPE], apply_axial_rope(k[..., MLA_NOPE:], ang_r, ang_c)], axis=-1)
    qh = q.transpose(0, 2, 1, 3)[:, :, None]
    o = block_attention(qh, k.transpose(0, 2, 1, 3), v.transpose(0, 2, 1, 3))
    return o[:, :, 0].transpose(0, 2, 1, 3).reshape(b, s, MLA_HEADS * MLA_V)


def linear_scan(a, bval):
    def combine(left, right):
        a1, b1 = left
        a2, b2 = right
        return a1 * a2, a2 * b1 + b2
    return lax.associative_scan(combine, (a, bval), axis=1)[1]


def rglru_direction(xc, wa, ba, wx, bx, lam, reverse):
    b, s, w = xc.shape
    xh = xc.reshape(b, s, LRU_BLOCKS, LRU_BLOCK_W)
    r = jax.nn.sigmoid(jnp.einsum('bsnc,ncd->bsnd', xh, wa).reshape(b, s, w) + ba)
    i = jax.nn.sigmoid(jnp.einsum('bsnc,ncd->bsnd', xh, wx).reshape(b, s, w) + bx)
    log_a = -LRU_C * r * jax.nn.softplus(-lam)
    a = jnp.exp(log_a)
    mult = jnp.sqrt(-jnp.expm1(2.0 * log_a))
    bval = mult * (i * xc)
    if reverse:
        return jnp.flip(linear_scan(jnp.flip(a, 1), jnp.flip(bval, 1)), 1)
    return linear_scan(a, bval)


def rglru_mixer(u, conv_w, conv_b, wa, ba, wx, bx, lam):
    gate, xb = jnp.split(u, 2, axis=-1)
    xc = dwconv(xb, conv_w, conv_b, left=LRU_CONV // 2)
    h = (rglru_direction(xc, wa[0], ba[0], wx[0], bx[0], lam[0], False)
         + rglru_direction(xc, wa[1], ba[1], wx[1], bx[1], lam[1], True))
    return jax.nn.gelu(gate) * h


def setup_inputs(seed: int = 0) -> dict:
    key = jax.random.key(seed)
    ks = iter(jax.random.split(key, 32))
    L, D = DEPTH, D_MODEL
    f32 = jnp.float32

    def nrm(shape, fan_in):
        return jax.random.normal(next(ks), shape, f32) * fan_in ** -0.5

    def gain(shape):
        return 1.0 + 0.05 * jax.random.normal(next(ks), shape, f32)

    def bias(shape, scale=0.02):
        return scale * jax.random.normal(next(ks), shape, f32)

    def lru_lambda(shape):
        a0 = jax.random.uniform(next(ks), shape, f32, minval=0.9, maxval=0.999)
        a_base = a0 ** (1.0 / LRU_C)
        return jnp.log(a_base) - jnp.log1p(-a_base)

    return {
        "x": jax.random.normal(next(ks), (BATCH, SEQ, D), f32),
        "ffn1_norm": gain((L, D)),
        "ffn1_wi": nrm((L, D, 2 * D_FF), D),
        "ffn1_wo": nrm((L, D_FF, D), D_FF),
        "mix_norm": gain((L, D)),
        "w_in": nrm((L, D, IN_TOTAL), D),
        "gqa_q_norm": gain((L, GQA_HEAD_DIM)),
        "gqa_k_norm": gain((L, GQA_HEAD_DIM)),
        "sc_conv_w": nrm((L, SC_KERNEL, SC_WIDTH), SC_KERNEL),
        "sc_conv_b": bias((L, SC_WIDTH)),
        "mla_qa_norm": gain((L, MLA_Q_LORA)),
        "mla_wq_up": nrm((L, MLA_Q_LORA, MLA_HEADS * (MLA_NOPE + MLA_ROPE)), MLA_Q_LORA),
        "mla_kva_norm": gain((L, MLA_KV_LORA)),
        "mla_wkv_up": nrm((L, MLA_KV_LORA, MLA_HEADS * (MLA_NOPE + MLA_V)), MLA_KV_LORA),
        "mla_q_norm": gain((L, MLA_NOPE + MLA_ROPE)),
        "mla_k_norm": gain((L, MLA_NOPE + MLA_ROPE)),
        "lru_conv_w": nrm((L, LRU_CONV, LRU_WIDTH), LRU_CONV),
        "lru_conv_b": bias((L, LRU_WIDTH)),
        "lru_wa": nrm((L, 2, LRU_BLOCKS, LRU_BLOCK_W, LRU_BLOCK_W), LRU_BLOCK_W),
        "lru_ba": bias((L, 2, LRU_WIDTH), 0.1),
        "lru_wx": nrm((L, 2, LRU_BLOCKS, LRU_BLOCK_W, LRU_BLOCK_W), LRU_BLOCK_W),
        "lru_bx": bias((L, 2, LRU_WIDTH), 0.1),
        "lru_lambda": lru_lambda((L, 2, LRU_WIDTH)),
        "w_branch": nrm((L, N_BRANCH, BRANCH_W, D), BRANCH_W),
        "w_out": nrm((L, D, D), D),
        "ffn2_norm": gain((L, D)),
        "ffn2_wi": nrm((L, D, 2 * D_FF), D),
        "ffn2_wo": nrm((L, D_FF, D), D_FF),
    }


def reference(x, ffn1_norm, ffn1_wi, ffn1_wo, mix_norm, w_in, gqa_q_norm, gqa_k_norm,
              sc_conv_w, sc_conv_b, mla_qa_norm, mla_wq_up, mla_kva_norm, mla_wkv_up,
              mla_q_norm, mla_k_norm, lru_conv_w, lru_conv_b, lru_wa, lru_ba, lru_wx,
              lru_bx, lru_lambda, w_branch, w_out, ffn2_norm, ffn2_wi, ffn2_wo):
    b, s, d = x.shape
    rows = s // GRID_W
    row_pos = jnp.repeat(jnp.arange(rows, dtype=jnp.float32), GRID_W)
    col_pos = jnp.tile(jnp.arange(GRID_W, dtype=jnp.float32), rows)
    gqa_ang_r = rope_angles(row_pos, GQA_HEAD_DIM // 2)
    gqa_ang_c = rope_angles(col_pos, GQA_HEAD_DIM // 2)
    mla_ang_r = rope_angles(row_pos, MLA_ROPE // 2)
    mla_ang_c = rope_angles(col_pos, MLA_ROPE // 2)
    split_idx = np.cumsum(IN_SIZES)[:-1].tolist()

    for l in range(DEPTH):
        x = x + 0.5 * swiglu_ffn(x, ffn1_norm[l], ffn1_wi[l], ffn1_wo[l])

        h = rmsnorm(x, mix_norm[l])
        (a_q, a_k, a_v, sc_u, c_qlat, c_kvlat, c_krope, lru_u, gate_lin) = jnp.split(
            h @ w_in[l], split_idx, axis=-1)
        o_a = gqa_mixer(a_q, a_k, a_v, gqa_q_norm[l], gqa_k_norm[l], gqa_ang_r, gqa_ang_c)
        o_b = shortconv_mixer(sc_u, sc_conv_w[l], sc_conv_b[l])
        o_c = mla_mixer(c_qlat, c_kvlat, c_krope, mla_qa_norm[l], mla_wq_up[l],
                        mla_kva_norm[l], mla_wkv_up[l], mla_q_norm[l], mla_k_norm[l],
                        mla_ang_r, mla_ang_c)
        o_d = rglru_mixer(lru_u, lru_conv_w[l], lru_conv_b[l], lru_wa[l], lru_ba[l],
                          lru_wx[l], lru_bx[l], lru_lambda[l])
        branches = jnp.stack([o_a, o_b, o_c, o_d], axis=2)
        y = jnp.einsum('bsnc,ncd->bsnd', branches, w_branch[l])
        g = jax.nn.sigmoid(gate_lin.reshape(b, s, N_BRANCH, d))
        merged = jnp.sum(g * y, axis=2)
        x = x + merged @ w_out[l]

        x = x + 0.5 * swiglu_ffn(x, ffn2_norm[l], ffn2_wi[l], ffn2_wo[l])
    return x
```

```python
import functools

import numpy as np
import jax
import jax.numpy as jnp
from jax import lax
from jax.experimental import pallas as pl
from jax.experimental.pallas import tpu as pltpu

F32 = jnp.float32
BF16 = jnp.bfloat16

GRID_W = 64
ROPE_THETA = 10000.0
EPS = 1e-6
GQA_HEADS = 8
GQA_KV_HEADS = 2
GQA_HEAD_DIM = 64
SC_WIDTH = 512
MLA_HEADS = 8
MLA_Q_LORA = 384
MLA_KV_LORA = 256
MLA_NOPE = 64
MLA_ROPE = 32
MLA_V = 64
LRU_WIDTH = 512
LRU_C = 8.0
N_BRANCH = 4
LOG2E = 1.4426950408889634

VMEM_LIMIT_BYTES = 56 * 1024 * 1024
HALO = 8
QK_PAD = 128
NEG_BIG = -1e30


def _params(*sem):
    return pltpu.CompilerParams(dimension_semantics=sem, vmem_limit_bytes=VMEM_LIMIT_BYTES)


def _const_spec(shape):
    zeros = (0,) * len(shape)
    return pl.BlockSpec(shape, lambda *_: zeros, pipeline_mode=pl.Buffered(1))


def _rms_rows(x, g):
    return x * lax.rsqrt(jnp.mean(x * x, axis=-1, keepdims=True) + EPS) * g


def _dot(a, b):
    return jnp.dot(a, b, preferred_element_type=F32)


def _dot_nt(a, b):
    return lax.dot_general(a, b, (((1,), (1,)), ((), ())), preferred_element_type=F32)


def _dot_tn(a, b):
    return lax.dot_general(a, b, (((0,), (0,)), ((), ())), preferred_element_type=F32)


def _ffn_body(x_ref, g_ref, wg_ref, wu_ref, wo_ref, o_ref, *, n_chunks):
    x = x_ref[...]
    h = _rms_rows(x, g_ref[...]).astype(BF16)
    tf = wg_ref.shape[1] // n_chunks
    acc = jnp.zeros(x.shape, F32)
    for c in range(n_chunks):
        gate = _dot(h, wg_ref[:, c * tf:(c + 1) * tf])
        up = _dot(h, wu_ref[:, c * tf:(c + 1) * tf])
        a = (gate * jax.nn.sigmoid(gate) * up).astype(BF16)
        acc = acc + _dot(a, wo_ref[c * tf:(c + 1) * tf, :])
    o_ref[...] = x + 0.5 * acc


def _ffn(x2d, g, wi, wo, *, tm):
    t, d = x2d.shape
    d_ff = wo.shape[0]
    wg = wi[:, :d_ff].astype(BF16)
    wu = wi[:, d_ff:].astype(BF16)
    n_chunks = 2 if d_ff % 256 == 0 else 1
    return pl.pallas_call(
        functools.partial(_ffn_body, n_chunks=n_chunks),
        grid=(t // tm,),
        in_specs=[
            pl.BlockSpec((tm, d), lambda i: (i, 0)),
            _const_spec((1, d)),
            _const_spec((d, d_ff)),
            _const_spec((d, d_ff)),
            _const_spec((d_ff, d)),
        ],
        out_specs=pl.BlockSpec((tm, d), lambda i: (i, 0)),
        out_shape=jax.ShapeDtypeStruct((t, d), F32),
        compiler_params=_params("parallel"),
        name="ffn",
    )(x2d, g.reshape(1, d), wg, wu, wo.astype(BF16))


def _rope_tables(s, dim):
    half = dim // 2
    inv = (np.float32(ROPE_THETA) ** (-np.arange(0, half, 2, dtype=np.float32) / np.float32(half))).astype(np.float32)
    t = np.arange(s)
    row = (t // GRID_W).astype(np.float32)
    col = (t % GRID_W).astype(np.float32)
    ang = np.concatenate([inv[:, None] * row[None, :], inv[:, None] * col[None, :]], axis=0)
    ang = ang.astype(np.float32).astype(np.float64)
    return jnp.asarray(np.cos(ang), F32), jnp.asarray(np.sin(ang), F32)


def _axial_rope(t, cos, sin):
    q = t.shape[1] // 4
    x1r, x2r, x1c, x2c = t[:, 0:q], t[:, q:2 * q], t[:, 2 * q:3 * q], t[:, 3 * q:4 * q]
    cr, sr = cos[None, 0:q], sin[None, 0:q]
    cc, sc = cos[None, q:2 * q], sin[None, q:2 * q]
    return jnp.concatenate(
        [x1r * cr - x2r * sr, x2r * cr + x1r * sr, x1c * cc - x2c * sc, x2c * cc + x1c * sc], axis=1)


def _gqa_prep_body(x_ref, g_ref, wt_ref, qg_ref, kg_ref, cos_ref, sin_ref, q_out, k_out, v_out):
    x = x_ref[0]
    tm = x.shape[0]
    h = _rms_rows(x, g_ref[...]).astype(BF16)
    y = _dot_nt(wt_ref[...], h)
    cos = cos_ref[...]
    sin = sin_ref[...]
    nq = GQA_HEADS * GQA_HEAD_DIM
    nk = GQA_KV_HEADS * GQA_HEAD_DIM

    def norm_rope(t, gain):
        r = lax.rsqrt(jnp.mean(t * t, axis=1, keepdims=True) + EPS)
        return _axial_rope(t * r * gain[None], cos, sin)

    q = norm_rope(y[0:nq].reshape(GQA_HEADS, GQA_HEAD_DIM, tm), qg_ref[...])
    q = q * (GQA_HEAD_DIM ** -0.5 * LOG2E)
    grp = GQA_HEADS // GQA_KV_HEADS
    z = jnp.zeros((grp, GQA_HEAD_DIM, tm), F32)
    qp = jnp.concatenate(
        [jnp.concatenate([q[:grp], z], axis=1), jnp.concatenate([z, q[grp:]], axis=1)], axis=0)
    q_out[0] = qp.astype(BF16)
    k = norm_rope(y[nq:nq + nk].reshape(GQA_KV_HEADS, GQA_HEAD_DIM, tm), kg_ref[...])
    k_out[0, 0] = k.reshape(nk, tm).T.astype(BF16)
    v_out[0] = y[nq + nk:nq + 2 * nk].reshape(GQA_KV_HEADS, GQA_HEAD_DIM, tm).astype(BF16)


def _gqa_prep(x, g, w_qkv, qg, kg, cos, sin, *, tm):
    b, s, d = x.shape
    n = w_qkv.shape[1]
    return pl.pallas_call(
        _gqa_prep_body,
        grid=(b, s // tm),
        in_specs=[
            pl.BlockSpec((1, tm, d), lambda bi, i: (bi, i, 0)),
            _const_spec((1, d)),
            _const_spec((n, d)),
            _const_spec((GQA_HEAD_DIM, 1)),
            _const_spec((GQA_HEAD_DIM, 1)),
            pl.BlockSpec((GQA_HEAD_DIM // 2, tm), lambda bi, i: (0, i)),
            pl.BlockSpec((GQA_HEAD_DIM // 2, tm), lambda bi, i: (0, i)),
        ],
        out_specs=[
            pl.BlockSpec((1, GQA_HEADS, QK_PAD, tm), lambda bi, i: (bi, 0, 0, i)),
            pl.BlockSpec((1, 1, tm, QK_PAD), lambda bi, i: (bi, 0, i, 0)),
            pl.BlockSpec((1, GQA_KV_HEADS, GQA_HEAD_DIM, tm), lambda bi, i: (bi, 0, 0, i)),
        ],
        out_shape=[
            jax.ShapeDtypeStruct((b, GQA_HEADS, QK_PAD, s), BF16),
            jax.ShapeDtypeStruct((b, 1, s, QK_PAD), BF16),
            jax.ShapeDtypeStruct((b, GQA_KV_HEADS, GQA_HEAD_DIM, s), BF16),
        ],
        compiler_params=_params("parallel", "parallel"),
        name="gqa_prep",
    )(x, g.reshape(1, d), w_qkv.T.astype(BF16), qg.reshape(-1, 1), kg.reshape(-1, 1), cos, sin)


def _mla_prep_body(x_ref, g_ref, wlat_ref, wkr_ref, qan_ref, kvan_ref, wqt_ref, wkvt_ref,
                   qg_ref, kg_ref, cos_ref, sin_ref, q_out, k_out, v_out):
    x = x_ref[0]
    tm = x.shape[0]
    h = _rms_rows(x, g_ref[...]).astype(BF16)
    lat = _dot(h, wlat_ref[...])
    qn = _rms_rows(lat[:, :MLA_Q_LORA], qan_ref[...]).astype(BF16)
    kvn = _rms_rows(lat[:, MLA_Q_LORA:], kvan_ref[...]).astype(BF16)
    dqk = MLA_NOPE + MLA_ROPE
    q = _dot_nt(wqt_ref[...], qn).reshape(MLA_HEADS, dqk, tm)
    kv = _dot_nt(wkvt_ref[...], kvn).reshape(MLA_HEADS, MLA_NOPE + MLA_V, tm)
    kr = _dot_nt(wkr_ref[...], h)
    cos = cos_ref[...]
    sin = sin_ref[...]
    pad = jnp.zeros((MLA_HEADS, QK_PAD - dqk, tm), F32)

    qg = qg_ref[...]
    rq = lax.rsqrt(jnp.mean(q * q, axis=1, keepdims=True) + EPS)
    qs = q * rq * (qg[None] * (dqk ** -0.5 * LOG2E))
    qf = jnp.concatenate([qs[:, :MLA_NOPE], _axial_rope(qs[:, MLA_NOPE:], cos, sin), pad], axis=1)
    q_out[0] = qf.astype(BF16)

    kg = kg_ref[...]
    k_nope = kv[:, :MLA_NOPE]
    ss = jnp.sum(k_nope * k_nope, axis=1, keepdims=True) + jnp.sum(kr * kr, axis=0, keepdims=True)[None]
    rk = lax.rsqrt(ss * (1.0 / dqk) + EPS)
    kn = k_nope * rk * kg[None, :MLA_NOPE]
    krn = _axial_rope(kr[None] * rk * kg[None, MLA_NOPE:], cos, sin)
    kf = jnp.concatenate([kn, krn, pad], axis=1)
    for hh in range(MLA_HEADS):
        k_out[0, hh] = kf[hh].T.astype(BF16)
    v_out[0] = kv[:, MLA_NOPE:].astype(BF16)


def _mla_prep(x, g, w_lat, w_kr, qan, kvan, wq_up, wkv_up, qg, kg, cos, sin, *, tm):
    b, s, d = x.shape
    dqk = MLA_NOPE + MLA_ROPE
    return pl.pallas_call(
        _mla_prep_body,
        grid=(b, s // tm),
        in_specs=[
            pl.BlockSpec((1, tm, d), lambda bi, i: (bi, i, 0)),
            _const_spec((1, d)),
            _const_spec((d, MLA_Q_LORA + MLA_KV_LORA)),
            _const_spec((MLA_ROPE, d)),
            _const_spec((1, MLA_Q_LORA)),
            _const_spec((1, MLA_KV_LORA)),
            _const_spec((MLA_HEADS * dqk, MLA_Q_LORA)),
            _const_spec((MLA_HEADS * (MLA_NOPE + MLA_V), MLA_KV_LORA)),
            _const_spec((dqk, 1)),
            _const_spec((dqk, 1)),
            pl.BlockSpec((MLA_ROPE // 2, tm), lambda bi, i: (0, i)),
            pl.BlockSpec((MLA_ROPE // 2, tm), lambda bi, i: (0, i)),
        ],
        out_specs=[
            pl.BlockSpec((1, MLA_HEADS, QK_PAD, tm), lambda bi, i: (bi, 0, 0, i)),
            pl.BlockSpec((1, MLA_HEADS, tm, QK_PAD), lambda bi, i: (bi, 0, i, 0)),
            pl.BlockSpec((1, MLA_HEADS, MLA_V, tm), lambda bi, i: (bi, 0, 0, i)),
        ],
        out_shape=[
            jax.ShapeDtypeStruct((b, MLA_HEADS, QK_PAD, s), BF16),
            jax.ShapeDtypeStruct((b, MLA_HEADS, s, QK_PAD), BF16),
            jax.ShapeDtypeStruct((b, MLA_HEADS, MLA_V, s), BF16),
        ],
        compiler_params=_params("parallel", "parallel"),
        name="mla_prep",
    )(x, g.reshape(1, d), w_lat.astype(BF16), w_kr.T.astype(BF16), qan.reshape(1, -1),
      kvan.reshape(1, -1), wq_up.T.astype(BF16), wkv_up.T.astype(BF16),
      qg.reshape(-1, 1), kg.reshape(-1, 1), cos, sin)


def _attn_body(q_ref, k_ref, v_ref, o_ref, m_scr, l_scr, acc_scr):
    ki = pl.program_id(3)

    @pl.when(ki == 0)
    def _():
        m_scr[...] = jnp.full(m_scr.shape, NEG_BIG, F32)
        l_scr[...] = jnp.zeros(l_scr.shape, F32)
        acc_scr[...] = jnp.zeros(acc_scr.shape, F32)

    s = _dot(k_ref[0, 0], q_ref[0, 0])
    m_prev = m_scr[...]
    m_new = jnp.maximum(m_prev, jnp.max(s, axis=0, keepdims=True))
    alpha = jnp.exp2(m_prev - m_new)
    p = jnp.exp2(s - m_new)
    l_scr[...] = alpha * l_scr[...] + jnp.sum(p, axis=0, keepdims=True)
    acc_scr[...] = alpha * acc_scr[...] + _dot(v_ref[0, 0], p.astype(BF16))
    m_scr[...] = m_new

    @pl.when(ki == pl.num_programs(3) - 1)
    def _():
        o_ref[0, 0] = (acc_scr[...] / l_scr[...]).astype(o_ref.dtype)


def _attention(qt, k, vt, *, tq, tk):
    b, nh, _, s = qt.shape
    kdiv = nh // k.shape[1]
    vdiv = nh // vt.shape[1]
    dv = vt.shape[2]
    return pl.pallas_call(
        _attn_body,
        grid=(b, nh, s // tq, s // tk),
        in_specs=[
            pl.BlockSpec((1, 1, QK_PAD, tq), lambda bi, h, qi, ki: (bi, h, 0, qi)),
            pl.BlockSpec((1, 1, tk, QK_PAD), lambda bi, h, qi, ki: (bi, h // kdiv, ki, 0)),
            pl.BlockSpec((1, 1, dv, tk), lambda bi, h, qi, ki: (bi, h // vdiv, 0, ki)),
        ],
        out_specs=pl.BlockSpec((1, 1, dv, tq), lambda bi, h, qi, ki: (bi, h, 0, qi)),
        out_shape=jax.ShapeDtypeStruct((b, nh, dv, s), BF16),
        scratch_shapes=[
            pltpu.VMEM((1, tq), F32),
            pltpu.VMEM((1, tq), F32),
            pltpu.VMEM((dv, tq), F32),
        ],
        compiler_params=_params("parallel", "parallel", "parallel", "arbitrary"),
        name="attention",
    )(qt, k, vt)


def _halo_specs(tm, d, n_tiles):
    nb = tm // HALO
    return [
        pl.BlockSpec((1, tm, d), lambda bi, i: (bi, i, 0)),
        pl.BlockSpec((1, HALO, d), lambda bi, i: (bi, jnp.maximum(i * nb - 1, 0), 0)),
        pl.BlockSpec((1, HALO, d), lambda bi, i: (bi, jnp.minimum((i + 1) * nb, n_tiles * nb - 1), 0)),
    ]


def _halo_rows_valid(tm, i, n_tiles):
    r = lax.broadcasted_iota(jnp.int32, (tm + 2 * HALO, 1), 0)
    lo = jnp.where(i == 0, HALO, 0)
    hi = jnp.where(i == n_tiles - 1, tm + HALO, tm + 2 * HALO)
    return ((r >= lo) & (r < hi)).astype(F32)


def _shortconv_body(xm_ref, xp_ref, xn_ref, g_ref, w_ref, cw_ref, cb_ref, o_ref, z_scr):
    i = pl.program_id(1)
    tm = xm_ref.shape[1]
    x = jnp.concatenate([xp_ref[0], xm_ref[0], xn_ref[0]], axis=0)
    h = _rms_rows(x, g_ref[...]).astype(BF16)
    u = _dot(h, w_ref[...])
    w = SC_WIDTH
    z_scr[...] = u[:, w:2 * w] * u[:, 2 * w:] * _halo_rows_valid(tm, i, pl.num_programs(1))
    cw = cw_ref[...]
    conv = cb_ref[...]
    for tap in range(cw.shape[0]):
        conv = conv + cw[tap:tap + 1] * z_scr[pl.ds(HALO - 1 + tap, tm), :]
    o_ref[0] = (u[HALO:HALO + tm, :w] * conv).astype(o_ref.dtype)


def _shortconv(x, g, w_sc, cw, cb, *, tm):
    b, s, d = x.shape
    return pl.pallas_call(
        _shortconv_body,
        grid=(b, s // tm),
        in_specs=_halo_specs(tm, d, s // tm) + [
            _const_spec((1, d)),
            _const_spec((d, 3 * SC_WIDTH)),
            _const_spec(cw.shape),
            _const_spec((1, SC_WIDTH)),
        ],
        out_specs=pl.BlockSpec((1, tm, SC_WIDTH), lambda bi, i: (bi, i, 0)),
        out_shape=jax.ShapeDtypeStruct((b, s, SC_WIDTH), BF16),
        scratch_shapes=[pltpu.VMEM((tm + 2 * HALO, SC_WIDTH), F32)],
        compiler_params=_params("parallel", "parallel"),
        name="shortconv",
    )(x, x, x, g.reshape(1, d), w_sc.astype(BF16), cw, cb.reshape(1, -1))


def _softplus(z):
    return jnp.maximum(z, 0.0) + jnp.log1p(jnp.exp(-jnp.abs(z)))


def _lru_gates(xm_ref, xp_ref, xn_ref, g_ref, w_ref, cw_ref, cb_ref, wa_ref, ba_ref, wx_ref, bx_ref,
               lam_ref, xb_scr, a_scr, b_scr, tile, n_tiles, d):
    tm = xm_ref.shape[1]
    x = jnp.concatenate([xp_ref[0], xm_ref[0], xn_ref[0]], axis=0)
    h = _rms_rows(x, g_ref[...]).astype(BF16)
    xb_scr[...] = _dot(h, w_ref[...]) * _halo_rows_valid(tm, tile, n_tiles)
    cw = cw_ref[...]
    xc = cb_ref[...]
    for tap in range(cw.shape[0]):
        xc = xc + cw[tap:tap + 1] * xb_scr[pl.ds(HALO - 2 + tap, tm), :]
    xcb = xc.astype(BF16)
    r = jax.nn.sigmoid(_dot(xcb, wa_ref[d]) + ba_ref[d:d + 1])
    ig = jax.nn.sigmoid(_dot(xcb, wx_ref[d]) + bx_ref[d:d + 1])
    log_a = (-LRU_C) * r * _softplus(-lam_ref[d:d + 1])
    a = jnp.exp(log_a)
    mult = jnp.sqrt(-jnp.tanh(log_a) * (a * a + 1.0))
    a_scr[...] = a
    b_scr[...] = mult * (ig * xc)


def _lru_body(xmf_ref, xpf_ref, xnf_ref, xmb_ref, xpb_ref, xnb_ref, g_ref, w_ref, cw_ref, cb_ref,
              wa_ref, ba_ref, wx_ref, bx_ref, lam_ref, hf_ref, hb_ref,
              xb_scr, af_scr, bf_scr, ab_scr, bb_scr, cf_scr, cbk_scr):
    i = pl.program_id(1)
    n_tiles = pl.num_programs(1)
    tm = xmf_ref.shape[1]

    @pl.when(i == 0)
    def _():
        cf_scr[...] = jnp.zeros(cf_scr.shape, F32)
        cbk_scr[...] = jnp.zeros(cbk_scr.shape, F32)

    shared = (g_ref, w_ref, cw_ref, cb_ref, wa_ref, ba_ref, wx_ref, bx_ref, lam_ref, xb_scr)
    _lru_gates(xmf_ref, xpf_ref, xnf_ref, *shared, af_scr, bf_scr, i, n_tiles, 0)
    _lru_gates(xmb_ref, xpb_ref, xnb_ref, *shared, ab_scr, bb_scr, n_tiles - 1 - i, n_tiles, 1)

    row = lax.broadcasted_iota(jnp.int32, (HALO, LRU_WIDTH), 0)
    nblk = tm // HALO

    def block(j, carry):
        cf, cbk = carry
        jf = pl.multiple_of(j * HALO, HALO)
        jb = pl.multiple_of((nblk - 1 - j) * HALO, HALO)
        a, bv = af_scr[pl.ds(jf, HALO), :], bf_scr[pl.ds(jf, HALO), :]
        a2, bv2 = ab_scr[pl.ds(jb, HALO), :], bb_scr[pl.ds(jb, HALO), :]
        for sh in (1, 2, 4):
            keep = row >= sh
            a_s = jnp.where(keep, pltpu.roll(a, sh, 0), 1.0)
            b_s = jnp.where(keep, pltpu.roll(bv, sh, 0), 0.0)
            bv = a * b_s + bv
            a = a * a_s
            keep2 = row < HALO - sh
            a2_s = jnp.where(keep2, pltpu.roll(a2, HALO - sh, 0), 1.0)
            b2_s = jnp.where(keep2, pltpu.roll(bv2, HALO - sh, 0), 0.0)
            bv2 = a2 * b2_s + bv2
            a2 = a2 * a2_s
        hf = a * cf + bv
        hb = a2 * cbk + bv2
        hf_ref[0, pl.ds(jf, HALO), :] = hf
        hb_ref[0, pl.ds(jb, HALO), :] = hb
        return hf[HALO - 1:HALO], hb[0:1]

    cf, cbk = lax.fori_loop(0, nblk, block, (cf_scr[...], cbk_scr[...]))
    cf_scr[...] = cf
    cbk_scr[...] = cbk


def _lru(x, g, w_x, cw, cb, wa_bd, ba, wx_bd, bx, lam, *, tm):
    b, s, d = x.shape
    n = s // tm
    nb = tm // HALO
    w = LRU_WIDTH
    fwd = _halo_specs(tm, d, n)
    bwd = [
        pl.BlockSpec((1, tm, d), lambda bi, i: (bi, n - 1 - i, 0)),
        pl.BlockSpec((1, HALO, d), lambda bi, i: (bi, jnp.maximum((n - 1 - i) * nb - 1, 0), 0)),
        pl.BlockSpec((1, HALO, d), lambda bi, i: (bi, jnp.minimum((n - i) * nb, n * nb - 1), 0)),
    ]
    return pl.pallas_call(
        _lru_body,
        grid=(b, n),
        in_specs=fwd + bwd + [
            _const_spec((1, d)),
            _const_spec((d, w)),
            _const_spec(cw.shape),
            _const_spec((1, w)),
            _const_spec((2, w, w)),
            _const_spec((2, w)),
            _const_spec((2, w, w)),
            _const_spec((2, w)),
            _const_spec((2, w)),
        ],
        out_specs=[
            pl.BlockSpec((1, tm, w), lambda bi, i: (bi, i, 0)),
            pl.BlockSpec((1, tm, w), lambda bi, i: (bi, n - 1 - i, 0)),
        ],
        out_shape=[jax.ShapeDtypeStruct((b, s, w), F32)] * 2,
        scratch_shapes=[pltpu.VMEM((tm + 2 * HALO, w), F32)]
        + [pltpu.VMEM((tm, w), F32)] * 4
        + [pltpu.VMEM((1, w), F32)] * 2,
        compiler_params=_params("parallel", "arbitrary"),
        name="rglru",
    )(x, x, x, x, x, x, g.reshape(1, d), w_x.astype(BF16), cw, cb.reshape(1, -1),
      wa_bd.astype(BF16), ba, wx_bd.astype(BF16), bx, lam)


def _block_diag(wblk):
    two, n, c, _ = wblk.shape
    eye = jnp.eye(n, dtype=wblk.dtype)
    return jnp.einsum("dnij,nm->dnimj", wblk, eye).reshape(two, n * c, n * c)


def _gelu_tanh(x):
    return 0.5 * x * (1.0 + jnp.tanh(0.7978845608028654 * (x + 0.044715 * (x * x * x))))


def _merge_body(x_ref, g_ref, oa_ref, ob_ref, oc_ref, hf_ref, hb_ref, wgate_ref, wlg_ref, wbr_ref,
                wout_ref, o_ref):
    x = x_ref[0]
    d = x.shape[1]
    h = _rms_rows(x, g_ref[...]).astype(BF16)
    od = (_gelu_tanh(_dot(h, wlg_ref[...])) * (hf_ref[0] + hb_ref[0])).astype(BF16)
    ys = (
        _dot_tn(oa_ref[0], wbr_ref[0]),
        _dot(ob_ref[0], wbr_ref[1]),
        _dot_tn(oc_ref[0], wbr_ref[2]),
        _dot(od, wbr_ref[3]),
    )
    merged = jnp.zeros(x.shape, F32)
    for n, y in enumerate(ys):
        gate = jax.nn.sigmoid(_dot(h, wgate_ref[:, n * d:(n + 1) * d]))
        merged = merged + gate * y
    o_ref[0] = x + _dot(merged.astype(BF16), wout_ref[...])


def _merge(x, g, oa_t, ob, oc_t, hf, hb, w_gate, w_lg, w_branch, w_out, *, tm):
    b, s, d = x.shape
    w = oa_t.shape[1]
    row = lambda width: pl.BlockSpec((1, tm, width), lambda bi, i: (bi, i, 0))
    col = pl.BlockSpec((1, w, tm), lambda bi, i: (bi, 0, i))
    return pl.pallas_call(
        _merge_body,
        grid=(b, s // tm),
        in_specs=[
            row(d), _const_spec((1, d)), col, row(w), col, row(w), row(w),
            _const_spec((d, N_BRANCH * d)),
            _const_spec((d, w)),
            _const_spec((N_BRANCH, w, d)),
            _const_spec((d, d)),
        ],
        out_specs=row(d),
        out_shape=jax.ShapeDtypeStruct((b, s, d), F32),
        compiler_params=_params("parallel", "parallel"),
        name="merge",
    )(x, g.reshape(1, d), oa_t, ob, oc_t, hf, hb, w_gate.astype(BF16), w_lg.astype(BF16),
      w_branch.astype(BF16), w_out.astype(BF16))


def _row_tile(n, target):
    t = min(n, target)
    while n % t:
        t //= 2
    return t


def kernel(x, ffn1_norm, ffn1_wi, ffn1_wo, mix_norm, w_in, gqa_q_norm, gqa_k_norm, sc_conv_w, sc_conv_b, mla_qa_norm, mla_wq_up, mla_kva_norm, mla_wkv_up, mla_q_norm, mla_k_norm, lru_conv_w, lru_conv_b, lru_wa, lru_ba, lru_wx, lru_bx, lru_lambda, w_branch, w_out, ffn2_norm, ffn2_wi, ffn2_wo):
    b, s, d = x.shape
    depth = ffn1_wi.shape[0]
    tm = _row_tile(s, 512)
    tq = _row_tile(s, 512)
    tk = _row_tile(s, 512)
    gqa_cos, gqa_sin = _rope_tables(s, GQA_HEAD_DIM)
    mla_cos, mla_sin = _rope_tables(s, MLA_ROPE)

    n_q = GQA_HEADS * GQA_HEAD_DIM
    n_kv = GQA_KV_HEADS * GQA_HEAD_DIM
    o_sc = n_q + 2 * n_kv
    o_qlat = o_sc + 3 * SC_WIDTH
    o_kr = o_qlat + MLA_Q_LORA + MLA_KV_LORA
    o_lru = o_kr + MLA_ROPE
    o_gate = o_lru + 2 * LRU_WIDTH

    for l in range(depth):
        x = _ffn(x.reshape(b * s, d), ffn1_norm[l], ffn1_wi[l], ffn1_wo[l], tm=tm).reshape(b, s, d)

        wl = w_in[l]
        g = mix_norm[l]
        qt, kk, vt = _gqa_prep(x, g, wl[:, :o_sc], gqa_q_norm[l], gqa_k_norm[l], gqa_cos, gqa_sin, tm=tm)
        oa_t = _attention(qt, kk, vt, tq=tq, tk=tk).reshape(b, GQA_HEADS * GQA_HEAD_DIM, s)
        qt, kk, vt = _mla_prep(x, g, wl[:, o_qlat:o_kr], wl[:, o_kr:o_lru], mla_qa_norm[l],
                               mla_kva_norm[l], mla_wq_up[l], mla_wkv_up[l], mla_q_norm[l],
                               mla_k_norm[l], mla_cos, mla_sin, tm=tm)
        oc_t = _attention(qt, kk, vt, tq=tq, tk=tk).reshape(b, MLA_HEADS * MLA_V, s)
        ob = _shortconv(x, g, wl[:, o_sc:o_qlat], sc_conv_w[l], sc_conv_b[l], tm=tm)
        hf, hb = _lru(x, g, wl[:, o_lru + LRU_WIDTH:o_gate], lru_conv_w[l], lru_conv_b[l],
                      _block_diag(lru_wa[l]), lru_ba[l], _block_diag(lru_wx[l]), lru_bx[l],
                      lru_lambda[l], tm=tm)
        x = _merge(x, g, oa_t, ob, oc_t, hf, hb, wl[:, o_gate:], wl[:, o_lru:o_lru + LRU_WIDTH],
                   w_branch[l], w_out[l], tm=tm)

        x = _ffn(x.reshape(b * s, d), ffn2_norm[l], ffn2_wi[l], ffn2_wo[l], tm=tm).reshape(b, s, d)
    return x
```

```python
import functools

import numpy as np
import jax
import jax.numpy as jnp
from jax import lax
from jax.experimental import pallas as pl
from jax.experimental.pallas import tpu as pltpu

F32 = jnp.float32
BF16 = jnp.bfloat16

GRID_W = 64
ROPE_THETA = 10000.0
EPS = 1e-6
GQA_HEADS = 8
GQA_KV_HEADS = 2
GQA_HEAD_DIM = 64
SC_WIDTH = 512
MLA_HEADS = 8
MLA_Q_LORA = 384
MLA_KV_LORA = 256
MLA_NOPE = 64
MLA_ROPE = 32
MLA_V = 64
LRU_WIDTH = 512
LRU_C = 8.0
N_BRANCH = 4
LOG2E = 1.4426950408889634

VMEM_LIMIT_BYTES = 56 * 1024 * 1024
HALO = 8
QK_PAD = 128
NEG_BIG = -1e30
MAX_UNSHIFTED_LOGIT = 64.0


def _params(*sem):
    return pltpu.CompilerParams(dimension_semantics=sem, vmem_limit_bytes=VMEM_LIMIT_BYTES)


def _const_spec(shape):
    zeros = (0,) * len(shape)
    return pl.BlockSpec(shape, lambda *_: zeros, pipeline_mode=pl.Buffered(1))


def _rms_rows(x, g):
    return x * lax.rsqrt(jnp.mean(x * x, axis=-1, keepdims=True) + EPS) * g


def _dot(a, b):
    return jnp.dot(a, b, preferred_element_type=F32)


def _dot_nt(a, b):
    return lax.dot_general(a, b, (((1,), (1,)), ((), ())), preferred_element_type=F32)


def _dot_tn(a, b):
    return lax.dot_general(a, b, (((0,), (0,)), ((), ())), preferred_element_type=F32)


def _ffn_body(x_ref, g_ref, wg_ref, wu_ref, wo_ref, o_ref, *, n_chunks):
    x = x_ref[...]
    h = _rms_rows(x, g_ref[...]).astype(BF16)
    tf = wg_ref.shape[1] // n_chunks
    acc = jnp.zeros(x.shape, F32)
    for c in range(n_chunks):
        gate = _dot(h, wg_ref[:, c * tf:(c + 1) * tf])
        up = _dot(h, wu_ref[:, c * tf:(c + 1) * tf])
        a = (gate * jax.nn.sigmoid(gate) * up).astype(BF16)
        acc = acc + _dot(a, wo_ref[c * tf:(c + 1) * tf, :])
    o_ref[...] = x + 0.5 * acc


def _ffn(x2d, g, wi, wo, *, tm):
    t, d = x2d.shape
    d_ff = wo.shape[0]
    wg = wi[:, :d_ff].astype(BF16)
    wu = wi[:, d_ff:].astype(BF16)
    n_chunks = 2 if d_ff % 256 == 0 else 1
    return pl.pallas_call(
        functools.partial(_ffn_body, n_chunks=n_chunks),
        grid=(t // tm,),
        in_specs=[
            pl.BlockSpec((tm, d), lambda i: (i, 0)),
            _const_spec((1, d)),
            _const_spec((d, d_ff)),
            _const_spec((d, d_ff)),
            _const_spec((d_ff, d)),
        ],
        out_specs=pl.BlockSpec((tm, d), lambda i: (i, 0)),
        out_shape=jax.ShapeDtypeStruct((t, d), F32),
        compiler_params=_params("parallel"),
        name="ffn",
    )(x2d, g.reshape(1, d), wg, wu, wo.astype(BF16))


def _rope_tables(s, dim):
    half = dim // 2
    inv = (np.float32(ROPE_THETA) ** (-np.arange(0, half, 2, dtype=np.float32) / np.float32(half))).astype(np.float32)
    t = np.arange(s)
    row = (t // GRID_W).astype(np.float32)
    col = (t % GRID_W).astype(np.float32)
    ang = np.concatenate([inv[:, None] * row[None, :], inv[:, None] * col[None, :]], axis=0)
    ang = ang.astype(np.float32).astype(np.float64)
    return jnp.asarray(np.cos(ang), F32), jnp.asarray(np.sin(ang), F32)


def _axial_rope(t, cos, sin):
    q = t.shape[1] // 4
    x1r, x2r, x1c, x2c = t[:, 0:q], t[:, q:2 * q], t[:, 2 * q:3 * q], t[:, 3 * q:4 * q]
    cr, sr = cos[None, 0:q], sin[None, 0:q]
    cc, sc = cos[None, q:2 * q], sin[None, q:2 * q]
    return jnp.concatenate(
        [x1r * cr - x2r * sr, x2r * cr + x1r * sr, x1c * cc - x2c * sc, x2c * cc + x1c * sc], axis=1)


def _gqa_prep_body(x_ref, g_ref, wt_ref, qg_ref, kg_ref, cos_ref, sin_ref, q_out, k_out, v_out):
    x = x_ref[0]
    tm = x.shape[0]
    h = _rms_rows(x, g_ref[...]).astype(BF16)
    y = _dot_nt(wt_ref[...], h)
    cos = cos_ref[...]
    sin = sin_ref[...]
    nq = GQA_HEADS * GQA_HEAD_DIM
    nk = GQA_KV_HEADS * GQA_HEAD_DIM

    def norm_rope(t, gain):
        r = lax.rsqrt(jnp.mean(t * t, axis=1, keepdims=True) + EPS)
        return _axial_rope(t * r * gain[None], cos, sin)

    q = norm_rope(y[0:nq].reshape(GQA_HEADS, GQA_HEAD_DIM, tm), qg_ref[...])
    q = q * (GQA_HEAD_DIM ** -0.5 * LOG2E)
    grp = GQA_HEADS // GQA_KV_HEADS
    z = jnp.zeros((grp, GQA_HEAD_DIM, tm), F32)
    qp = jnp.concatenate(
        [jnp.concatenate([q[:grp], z], axis=1), jnp.concatenate([z, q[grp:]], axis=1)], axis=0)
    q_out[0] = qp.astype(BF16)
    k = norm_rope(y[nq:nq + nk].reshape(GQA_KV_HEADS, GQA_HEAD_DIM, tm), kg_ref[...])
    k_out[0, 0, 0] = k.reshape(nk, tm).T.astype(BF16)
    v_out[0, :, 0] = y[nq + nk:nq + 2 * nk].reshape(GQA_KV_HEADS, GQA_HEAD_DIM, tm).astype(BF16)


def _gqa_prep(x, g, w_qkv, qg, kg, cos, sin, *, tm):
    b, s, d = x.shape
    n = w_qkv.shape[1]
    return pl.pallas_call(
        _gqa_prep_body,
        grid=(b, s // tm),
        in_specs=[
            pl.BlockSpec((1, tm, d), lambda bi, i: (bi, i, 0)),
            _const_spec((1, d)),
            _const_spec((n, d)),
            _const_spec((GQA_HEAD_DIM, 1)),
            _const_spec((GQA_HEAD_DIM, 1)),
            pl.BlockSpec((GQA_HEAD_DIM // 2, tm), lambda bi, i: (0, i)),
            pl.BlockSpec((GQA_HEAD_DIM // 2, tm), lambda bi, i: (0, i)),
        ],
        out_specs=[
            pl.BlockSpec((1, GQA_HEADS, QK_PAD, tm), lambda bi, i: (bi, 0, 0, i)),
            pl.BlockSpec((1, 1, 1, tm, QK_PAD), lambda bi, i: (bi, 0, i, 0, 0)),
            pl.BlockSpec((1, GQA_KV_HEADS, 1, GQA_HEAD_DIM, tm), lambda bi, i: (bi, 0, i, 0, 0)),
        ],
        out_shape=[
            jax.ShapeDtypeStruct((b, GQA_HEADS, QK_PAD, s), BF16),
            jax.ShapeDtypeStruct((b, 1, s // tm, tm, QK_PAD), BF16),
            jax.ShapeDtypeStruct((b, GQA_KV_HEADS, s // tm, GQA_HEAD_DIM, tm), BF16),
        ],
        compiler_params=_params("parallel", "parallel"),
        name="gqa_prep",
    )(x, g.reshape(1, d), w_qkv.T.astype(BF16), qg.reshape(-1, 1), kg.reshape(-1, 1), cos, sin)


def _mla_prep_body(x_ref, g_ref, wlat_ref, wkr_ref, qan_ref, kvan_ref, wqt_ref, wkvt_ref,
                   qg_ref, kg_ref, cos_ref, sin_ref, q_out, k_out, v_out):
    x = x_ref[0]
    tm = x.shape[0]
    h = _rms_rows(x, g_ref[...]).astype(BF16)
    lat = _dot(h, wlat_ref[...])
    qn = _rms_rows(lat[:, :MLA_Q_LORA], qan_ref[...]).astype(BF16)
    kvn = _rms_rows(lat[:, MLA_Q_LORA:], kvan_ref[...]).astype(BF16)
    dqk = MLA_NOPE + MLA_ROPE
    q = _dot_nt(wqt_ref[...], qn).reshape(MLA_HEADS, dqk, tm)
    kv = _dot_nt(wkvt_ref[...], kvn).reshape(MLA_HEADS, MLA_NOPE + MLA_V, tm)
    kr = _dot_nt(wkr_ref[...], h)
    cos = cos_ref[...]
    sin = sin_ref[...]
    pad = jnp.zeros((MLA_HEADS, QK_PAD - dqk, tm), F32)

    qg = qg_ref[...]
    rq = lax.rsqrt(jnp.mean(q * q, axis=1, keepdims=True) + EPS)
    qs = q * rq * (qg[None] * (dqk ** -0.5 * LOG2E))
    qf = jnp.concatenate([qs[:, :MLA_NOPE], _axial_rope(qs[:, MLA_NOPE:], cos, sin), pad], axis=1)
    q_out[0] = qf.astype(BF16)

    kg = kg_ref[...]
    k_nope = kv[:, :MLA_NOPE]
    ss = jnp.sum(k_nope * k_nope, axis=1, keepdims=True) + jnp.sum(kr * kr, axis=0, keepdims=True)[None]
    rk = lax.rsqrt(ss * (1.0 / dqk) + EPS)
    kn = k_nope * rk * kg[None, :MLA_NOPE]
    krn = _axial_rope(kr[None] * rk * kg[None, MLA_NOPE:], cos, sin)
    kf = jnp.concatenate([kn, krn, pad], axis=1)
    for hh in range(MLA_HEADS):
        k_out[0, hh, 0] = kf[hh].T.astype(BF16)
    v_out[0, :, 0] = kv[:, MLA_NOPE:].astype(BF16)


def _mla_prep(x, g, w_lat, w_kr, qan, kvan, wq_up, wkv_up, qg, kg, cos, sin, *, tm):
    b, s, d = x.shape
    dqk = MLA_NOPE + MLA_ROPE
    return pl.pallas_call(
        _mla_prep_body,
        grid=(b, s // tm),
        in_specs=[
            pl.BlockSpec((1, tm, d), lambda bi, i: (bi, i, 0)),
            _const_spec((1, d)),
            _const_spec((d, MLA_Q_LORA + MLA_KV_LORA)),
            _const_spec((MLA_ROPE, d)),
            _const_spec((1, MLA_Q_LORA)),
            _const_spec((1, MLA_KV_LORA)),
            _const_spec((MLA_HEADS * dqk, MLA_Q_LORA)),
            _const_spec((MLA_HEADS * (MLA_NOPE + MLA_V), MLA_KV_LORA)),
            _const_spec((dqk, 1)),
            _const_spec((dqk, 1)),
            pl.BlockSpec((MLA_ROPE // 2, tm), lambda bi, i: (0, i)),
            pl.BlockSpec((MLA_ROPE // 2, tm), lambda bi, i: (0, i)),
        ],
        out_specs=[
            pl.BlockSpec((1, MLA_HEADS, QK_PAD, tm), lambda bi, i: (bi, 0, 0, i)),
            pl.BlockSpec((1, MLA_HEADS, 1, tm, QK_PAD), lambda bi, i: (bi, 0, i, 0, 0)),
            pl.BlockSpec((1, MLA_HEADS, 1, MLA_V, tm), lambda bi, i: (bi, 0, i, 0, 0)),
        ],
        out_shape=[
            jax.ShapeDtypeStruct((b, MLA_HEADS, QK_PAD, s), BF16),
            jax.ShapeDtypeStruct((b, MLA_HEADS, s // tm, tm, QK_PAD), BF16),
            jax.ShapeDtypeStruct((b, MLA_HEADS, s // tm, MLA_V, tm), BF16),
        ],
        compiler_params=_params("parallel", "parallel"),
        name="mla_prep",
    )(x, g.reshape(1, d), w_lat.astype(BF16), w_kr.T.astype(BF16), qan.reshape(1, -1),
      kvan.reshape(1, -1), wq_up.T.astype(BF16), wkv_up.T.astype(BF16),
      qg.reshape(-1, 1), kg.reshape(-1, 1), cos, sin)


def _attn_body(q_ref, k_ref, v_ref, o_ref, s0_scr, s1_scr, mx0_scr, mx1_scr, m_scr, l_scr, acc_scr,
               *, strip):
    tq = q_ref.shape[3]
    n_kt = k_ref.shape[2]
    strips = [slice(c * strip, (c + 1) * strip) for c in range(tq // strip)]
    slots = ((s0_scr, mx0_scr), (s1_scr, mx1_scr))
    m_scr[...] = jnp.full(m_scr.shape, NEG_BIG, F32)
    l_scr[...] = jnp.zeros(l_scr.shape, F32)
    acc_scr[...] = jnp.zeros(acc_scr.shape, F32)

    def scores(j, slot):
        s_scr, mx_scr = slots[slot]
        k = k_ref[0, 0, j]
        for sl in strips:
            s = _dot(k, q_ref[0, 0, :, sl])
            s_scr[:, sl] = s
            mx_scr[:, sl] = jnp.max(s, axis=0, keepdims=True)

    def accumulate(j, slot):
        s_scr, mx_scr = slots[slot]
        v = v_ref[0, 0, j]
        for sl in strips:
            m_prev = m_scr[:, sl]
            m_new = jnp.maximum(m_prev, mx_scr[:, sl])
            alpha = jnp.exp2(m_prev - m_new)
            p = jnp.exp2(s_scr[:, sl] - m_new)
            l_scr[:, sl] = alpha * l_scr[:, sl] + jnp.sum(p, axis=0, keepdims=True)
            acc_scr[:, sl] = alpha * acc_scr[:, sl] + _dot(v, p.astype(BF16))
            m_scr[:, sl] = m_new

    def pair(i, carry):
        j = 2 * i
        scores(j + 1, 1)
        accumulate(j, 0)
        scores(j + 2, 0)
        accumulate(j + 1, 1)
        return carry

    scores(0, 0)
    n_pairs = (n_kt - 1) // 2
    lax.fori_loop(0, n_pairs, pair, 0)
    if n_kt == 2 * n_pairs + 2:
        scores(n_kt - 1, 1)
        accumulate(n_kt - 2, 0)
        accumulate(n_kt - 1, 1)
    else:
        accumulate(n_kt - 1, 0)
    o_ref[0, 0] = (acc_scr[...] / l_scr[...]).astype(o_ref.dtype)


def _attn_bounded_body(q_ref, k_ref, v_ref, o_ref, p0_scr, p1_scr, l_scr, acc_scr, *, strip,
                       steps_per_iter):
    tq = q_ref.shape[3]
    n_kt, tk = k_ref.shape[2], k_ref.shape[3]
    strips = [slice(c * strip, (c + 1) * strip) for c in range(tq // strip)]
    p_slots = (p0_scr, p1_scr)
    l_scr[...] = jnp.zeros(l_scr.shape, F32)
    acc_scr[...] = jnp.zeros(acc_scr.shape, F32)

    def probs(j, slot, sl):
        p = jnp.exp2(_dot(k_ref[0, 0, j], q_ref[0, 0, :, sl]))
        l_scr[:, sl] += jnp.sum(p.reshape(tk // 8, 8, strip), axis=0)
        p_slots[slot][:, sl] = p.astype(BF16)

    def accumulate(j, slot, sl):
        acc_scr[:, sl] += _dot(v_ref[0, 0, j], p_slots[slot][:, sl])

    def step(j_probs, j_acc, slot_probs):
        for sl in strips:
            if j_probs is not None:
                probs(j_probs, slot_probs, sl)
            if j_acc is not None:
                accumulate(j_acc, 1 - slot_probs, sl)

    def group(i, carry):
        for u in range(steps_per_iter):
            j = i * steps_per_iter + u
            step(j + 1, j, (u + 1) % 2)
        return carry

    step(0, None, 0)
    n_iter = (n_kt - 1) // steps_per_iter
    lax.fori_loop(0, n_iter, group, 0)
    for j in range(n_iter * steps_per_iter, n_kt):
        step(j + 1 if j + 1 < n_kt else None, j, (j + 1) % 2)
    l = jnp.sum(l_scr[...], axis=0, keepdims=True)
    o_ref[0, 0] = (acc_scr[...] / l).astype(o_ref.dtype)


def _attention(qt, k, vt, logit_bound, *, tq, strip):
    b, nh, _, s = qt.shape
    _, nhk, n_kt, tk, _ = k.shape
    kdiv = nh // nhk
    vdiv = nh // vt.shape[1]
    dv = vt.shape[3]
    common = dict(
        grid=(b, nh, s // tq),
        in_specs=[
            pl.BlockSpec((1, 1, QK_PAD, tq), lambda bi, h, qi: (bi, h, 0, qi)),
            pl.BlockSpec((1, 1, n_kt, tk, QK_PAD), lambda bi, h, qi: (bi, h // kdiv, 0, 0, 0)),
            pl.BlockSpec((1, 1, n_kt, dv, tk), lambda bi, h, qi: (bi, h // vdiv, 0, 0, 0)),
        ],
        out_specs=pl.BlockSpec((1, 1, dv, tq), lambda bi, h, qi: (bi, h, 0, qi)),
        out_shape=jax.ShapeDtypeStruct((b, nh, dv, s), BF16),
        compiler_params=_params("parallel", "parallel", "parallel"),
    )
    online = pl.pallas_call(
        functools.partial(_attn_body, strip=strip),
        scratch_shapes=[
            pltpu.VMEM((tk, tq), F32),
            pltpu.VMEM((tk, tq), F32),
            pltpu.VMEM((1, tq), F32),
            pltpu.VMEM((1, tq), F32),
            pltpu.VMEM((1, tq), F32),
            pltpu.VMEM((1, tq), F32),
            pltpu.VMEM((dv, tq), F32),
        ],
        name="attention_online",
        **common,
    )
    bounded = pl.pallas_call(
        functools.partial(_attn_bounded_body, strip=strip, steps_per_iter=4),
        scratch_shapes=[
            pltpu.VMEM((tk, tq), BF16),
            pltpu.VMEM((tk, tq), BF16),
            pltpu.VMEM((8, tq), F32),
            pltpu.VMEM((dv, tq), F32),
        ],
        name="attention_bounded",
        **common,
    )
    return lax.cond(logit_bound <= MAX_UNSHIFTED_LOGIT, bounded, online, qt, k, vt)


def _logit_bound(q_gain, k_gain, head_dim):
    return (1.02 * LOG2E * head_dim ** 0.5) * jnp.max(jnp.abs(q_gain)) * jnp.max(jnp.abs(k_gain))


def _halo_specs(tm, d, n_tiles):
    nb = tm // HALO
    return [
        pl.BlockSpec((1, tm, d), lambda bi, i: (bi, i, 0)),
        pl.BlockSpec((1, HALO, d), lambda bi, i: (bi, jnp.maximum(i * nb - 1, 0), 0)),
        pl.BlockSpec((1, HALO, d), lambda bi, i: (bi, jnp.minimum((i + 1) * nb, n_tiles * nb - 1), 0)),
    ]


def _halo_rows_valid(tm, i, n_tiles):
    r = lax.broadcasted_iota(jnp.int32, (tm + 2 * HALO, 1), 0)
    lo = jnp.where(i == 0, HALO, 0)
    hi = jnp.where(i == n_tiles - 1, tm + HALO, tm + 2 * HALO)
    return ((r >= lo) & (r < hi)).astype(F32)


def _shortconv_body(xm_ref, xp_ref, xn_ref, g_ref, w_ref, cw_ref, cb_ref, o_ref, z_scr):
    i = pl.program_id(1)
    tm = xm_ref.shape[1]
    x = jnp.concatenate([xp_ref[0], xm_ref[0], xn_ref[0]], axis=0)
    h = _rms_rows(x, g_ref[...]).astype(BF16)
    u = _dot(h, w_ref[...])
    w = SC_WIDTH
    z_scr[...] = u[:, w:2 * w] * u[:, 2 * w:] * _halo_rows_valid(tm, i, pl.num_programs(1))
    cw = cw_ref[...]
    conv = cb_ref[...]
    for tap in range(cw.shape[0]):
        conv = conv + cw[tap:tap + 1] * z_scr[pl.ds(HALO - 1 + tap, tm), :]
    o_ref[0] = (u[HALO:HALO + tm, :w] * conv).astype(o_ref.dtype)


def _shortconv(x, g, w_sc, cw, cb, *, tm):
    b, s, d = x.shape
    return pl.pallas_call(
        _shortconv_body,
        grid=(b, s // tm),
        in_specs=_halo_specs(tm, d, s // tm) + [
            _const_spec((1, d)),
            _const_spec((d, 3 * SC_WIDTH)),
            _const_spec(cw.shape),
            _const_spec((1, SC_WIDTH)),
        ],
        out_specs=pl.BlockSpec((1, tm, SC_WIDTH), lambda bi, i: (bi, i, 0)),
        out_shape=jax.ShapeDtypeStruct((b, s, SC_WIDTH), BF16),
        scratch_shapes=[pltpu.VMEM((tm + 2 * HALO, SC_WIDTH), F32)],
        compiler_params=_params("parallel", "parallel"),
        name="shortconv",
    )(x, x, x, g.reshape(1, d), w_sc.astype(BF16), cw, cb.reshape(1, -1))


def _softplus(z):
    return jnp.maximum(z, 0.0) + jnp.log1p(jnp.exp(-jnp.abs(z)))


def _lru_gates(xm_ref, xp_ref, xn_ref, g_ref, w_ref, cw_ref, cb_ref, wa_ref, ba_ref, wx_ref, bx_ref,
               lam_ref, xb_scr, a_scr, b_scr, tile, n_tiles, d):
    tm = xm_ref.shape[1]
    x = jnp.concatenate([xp_ref[0], xm_ref[0], xn_ref[0]], axis=0)
    h = _rms_rows(x, g_ref[...]).astype(BF16)
    xb_scr[...] = _dot(h, w_ref[...]) * _halo_rows_valid(tm, tile, n_tiles)
    cw = cw_ref[...]
    xc = cb_ref[...]
    for tap in range(cw.shape[0]):
        xc = xc + cw[tap:tap + 1] * xb_scr[pl.ds(HALO - 2 + tap, tm), :]
    xcb = xc.astype(BF16)
    r = jax.nn.sigmoid(_dot(xcb, wa_ref[d]) + ba_ref[d:d + 1])
    ig = jax.nn.sigmoid(_dot(xcb, wx_ref[d]) + bx_ref[d:d + 1])
    log_a = (-LRU_C) * r * _softplus(-lam_ref[d:d + 1])
    a = jnp.exp(log_a)
    mult = jnp.sqrt(-jnp.tanh(log_a) * (a * a + 1.0))
    a_scr[...] = a
    b_scr[...] = mult * (ig * xc)


def _lru_body(xmf_ref, xpf_ref, xnf_ref, xmb_ref, xpb_ref, xnb_ref, g_ref, w_ref, cw_ref, cb_ref,
              wa_ref, ba_ref, wx_ref, bx_ref, lam_ref, hf_ref, hb_ref,
              xb_scr, af_scr, bf_scr, ab_scr, bb_scr, cf_scr, cbk_scr):
    i = pl.program_id(1)
    n_tiles = pl.num_programs(1)
    tm = xmf_ref.shape[1]

    @pl.when(i == 0)
    def _():
        cf_scr[...] = jnp.zeros(cf_scr.shape, F32)
        cbk_scr[...] = jnp.zeros(cbk_scr.shape, F32)

    shared = (g_ref, w_ref, cw_ref, cb_ref, wa_ref, ba_ref, wx_ref, bx_ref, lam_ref, xb_scr)
    _lru_gates(xmf_ref, xpf_ref, xnf_ref, *shared, af_scr, bf_scr, i, n_tiles, 0)
    _lru_gates(xmb_ref, xpb_ref, xnb_ref, *shared, ab_scr, bb_scr, n_tiles - 1 - i, n_tiles, 1)

    row = lax.broadcasted_iota(jnp.int32, (HALO, LRU_WIDTH), 0)
    nblk = tm // HALO

    def block(j, carry):
        cf, cbk = carry
        jf = pl.multiple_of(j * HALO, HALO)
        jb = pl.multiple_of((nblk - 1 - j) * HALO, HALO)
        a, bv = af_scr[pl.ds(jf, HALO), :], bf_scr[pl.ds(jf, HALO), :]
        a2, bv2 = ab_scr[pl.ds(jb, HALO), :], bb_scr[pl.ds(jb, HALO), :]
        for sh in (1, 2, 4):
            keep = row >= sh
            a_s = jnp.where(keep, pltpu.roll(a, sh, 0), 1.0)
            b_s = jnp.where(keep, pltpu.roll(bv, sh, 0), 0.0)
            bv = a * b_s + bv
            a = a * a_s
            keep2 = row < HALO - sh
            a2_s = jnp.where(keep2, pltpu.roll(a2, HALO - sh, 0), 1.0)
            b2_s = jnp.where(keep2, pltpu.roll(bv2, HALO - sh, 0), 0.0)
            bv2 = a2 * b2_s + bv2
            a2 = a2 * a2_s
        hf = a * cf + bv
        hb = a2 * cbk + bv2
        hf_ref[0, pl.ds(jf, HALO), :] = hf
        hb_ref[0, pl.ds(jb, HALO), :] = hb
        return hf[HALO - 1:HALO], hb[0:1]

    cf, cbk = lax.fori_loop(0, nblk, block, (cf_scr[...], cbk_scr[...]))
    cf_scr[...] = cf
    cbk_scr[...] = cbk


def _lru(x, g, w_x, cw, cb, wa_bd, ba, wx_bd, bx, lam, *, tm):
    b, s, d = x.shape
    n = s // tm
    nb = tm // HALO
    w = LRU_WIDTH
    fwd = _halo_specs(tm, d, n)
    bwd = [
        pl.BlockSpec((1, tm, d), lambda bi, i: (bi, n - 1 - i, 0)),
        pl.BlockSpec((1, HALO, d), lambda bi, i: (bi, jnp.maximum((n - 1 - i) * nb - 1, 0), 0)),
        pl.BlockSpec((1, HALO, d), lambda bi, i: (bi, jnp.minimum((n - i) * nb, n * nb - 1), 0)),
    ]
    return pl.pallas_call(
        _lru_body,
        grid=(b, n),
        in_specs=fwd + bwd + [
            _const_spec((1, d)),
            _const_spec((d, w)),
            _const_spec(cw.shape),
            _const_spec((1, w)),
            _const_spec((2, w, w)),
            _const_spec((2, w)),
            _const_spec((2, w, w)),
            _const_spec((2, w)),
            _const_spec((2, w)),
        ],
        out_specs=[
            pl.BlockSpec((1, tm, w), lambda bi, i: (bi, i, 0)),
            pl.BlockSpec((1, tm, w), lambda bi, i: (bi, n - 1 - i, 0)),
        ],
        out_shape=[jax.ShapeDtypeStruct((b, s, w), F32)] * 2,
        scratch_shapes=[pltpu.VMEM((tm + 2 * HALO, w), F32)]
        + [pltpu.VMEM((tm, w), F32)] * 4
        + [pltpu.VMEM((1, w), F32)] * 2,
        compiler_params=_params("parallel", "arbitrary"),
        name="rglru",
    )(x, x, x, x, x, x, g.reshape(1, d), w_x.astype(BF16), cw, cb.reshape(1, -1),
      wa_bd.astype(BF16), ba, wx_bd.astype(BF16), bx, lam)


def _block_diag(wblk):
    two, n, c, _ = wblk.shape
    eye = jnp.eye(n, dtype=wblk.dtype)
    return jnp.einsum("dnij,nm->dnimj", wblk, eye).reshape(two, n * c, n * c)


def _gelu_tanh(x):
    return 0.5 * x * (1.0 + jnp.tanh(0.7978845608028654 * (x + 0.044715 * (x * x * x))))


def _merge_body(x_ref, g_ref, oa_ref, ob_ref, oc_ref, hf_ref, hb_ref, wgate_ref, wlg_ref, wbr_ref,
                wout_ref, o_ref):
    x = x_ref[0]
    d = x.shape[1]
    h = _rms_rows(x, g_ref[...]).astype(BF16)
    od = (_gelu_tanh(_dot(h, wlg_ref[...])) * (hf_ref[0] + hb_ref[0])).astype(BF16)
    ys = (
        _dot_tn(oa_ref[0], wbr_ref[0]),
        _dot(ob_ref[0], wbr_ref[1]),
        _dot_tn(oc_ref[0], wbr_ref[2]),
        _dot(od, wbr_ref[3]),
    )
    merged = jnp.zeros(x.shape, F32)
    for n, y in enumerate(ys):
        gate = jax.nn.sigmoid(_dot(h, wgate_ref[:, n * d:(n + 1) * d]))
        merged = merged + gate * y
    o_ref[0] = x + _dot(merged.astype(BF16), wout_ref[...])


def _merge(x, g, oa_t, ob, oc_t, hf, hb, w_gate, w_lg, w_branch, w_out, *, tm):
    b, s, d = x.shape
    w = oa_t.shape[1]
    row = lambda width: pl.BlockSpec((1, tm, width), lambda bi, i: (bi, i, 0))
    col = pl.BlockSpec((1, w, tm), lambda bi, i: (bi, 0, i))
    return pl.pallas_call(
        _merge_body,
        grid=(b, s // tm),
        in_specs=[
            row(d), _const_spec((1, d)), col, row(w), col, row(w), row(w),
            _const_spec((d, N_BRANCH * d)),
            _const_spec((d, w)),
            _const_spec((N_BRANCH, w, d)),
            _const_spec((d, d)),
        ],
        out_specs=row(d),
        out_shape=jax.ShapeDtypeStruct((b, s, d), F32),
        compiler_params=_params("parallel", "parallel"),
        name="merge",
    )(x, g.reshape(1, d), oa_t, ob, oc_t, hf, hb, w_gate.astype(BF16), w_lg.astype(BF16),
      w_branch.astype(BF16), w_out.astype(BF16))


def _row_tile(n, target):
    t = min(n, target)
    while n % t:
        t //= 2
    return t


def kernel(x, ffn1_norm, ffn1_wi, ffn1_wo, mix_norm, w_in, gqa_q_norm, gqa_k_norm, sc_conv_w, sc_conv_b, mla_qa_norm, mla_wq_up, mla_kva_norm, mla_wkv_up, mla_q_norm, mla_k_norm, lru_conv_w, lru_conv_b, lru_wa, lru_ba, lru_wx, lru_bx, lru_lambda, w_branch, w_out, ffn2_norm, ffn2_wi, ffn2_wo):
    b, s, d = x.shape
    depth = ffn1_wi.shape[0]
    tm = _row_tile(s, 512)
    tq = _row_tile(s, 1024)
    strip = min(tq, 256)
    gqa_cos, gqa_sin = _rope_tables(s, GQA_HEAD_DIM)
    mla_cos, mla_sin = _rope_tables(s, MLA_ROPE)

    n_q = GQA_HEADS * GQA_HEAD_DIM
    n_kv = GQA_KV_HEADS * GQA_HEAD_DIM
    o_sc = n_q + 2 * n_kv
    o_qlat = o_sc + 3 * SC_WIDTH
    o_kr = o_qlat + MLA_Q_LORA + MLA_KV_LORA
    o_lru = o_kr + MLA_ROPE
    o_gate = o_lru + 2 * LRU_WIDTH

    for l in range(depth):
        x = _ffn(x.reshape(b * s, d), ffn1_norm[l], ffn1_wi[l], ffn1_wo[l], tm=tm).reshape(b, s, d)

        wl = w_in[l]
        g = mix_norm[l]
        qt, kk, vt = _gqa_prep(x, g, wl[:, :o_sc], gqa_q_norm[l], gqa_k_norm[l], gqa_cos, gqa_sin, tm=tm)
        oa_t = _attention(qt, kk, vt, _logit_bound(gqa_q_norm[l], gqa_k_norm[l], GQA_HEAD_DIM),
                          tq=tq, strip=strip).reshape(b, GQA_HEADS * GQA_HEAD_DIM, s)
        qt, kk, vt = _mla_prep(x, g, wl[:, o_qlat:o_kr], wl[:, o_kr:o_lru], mla_qa_norm[l],
                               mla_kva_norm[l], mla_wq_up[l], mla_wkv_up[l], mla_q_norm[l],
                               mla_k_norm[l], mla_cos, mla_sin, tm=tm)
        oc_t = _attention(qt, kk, vt, _logit_bound(mla_q_norm[l], mla_k_norm[l], MLA_NOPE + MLA_ROPE),
                          tq=tq, strip=strip).reshape(b, MLA_HEADS * MLA_V, s)
        ob = _shortconv(x, g, wl[:, o_sc:o_qlat], sc_conv_w[l], sc_conv_b[l], tm=tm)
        hf, hb = _lru(x, g, wl[:, o_lru + LRU_WIDTH:o_gate], lru_conv_w[l], lru_conv_b[l],
                      _block_diag(lru_wa[l]), lru_ba[l], _block_diag(lru_wx[l]), lru_bx[l],
                      lru_lambda[l], tm=tm)
        x = _merge(x, g, oa_t, ob, oc_t, hf, hb, wl[:, o_gate:], wl[:, o_lru:o_lru + LRU_WIDTH],
                   w_branch[l], w_out[l], tm=tm)

        x = _ffn(x.reshape(b * s, d), ffn2_norm[l], ffn2_wi[l], ffn2_wo[l], tm=tm).reshape(b, s, d)
    return x
```

```python
import functools

import numpy as np
import jax
import jax.numpy as jnp
from jax import lax
from jax.experimental import pallas as pl
from jax.experimental.pallas import tpu as pltpu

F32 = jnp.float32
BF16 = jnp.bfloat16

GRID_W = 64
ROPE_THETA = 10000.0
EPS = 1e-6
GQA_HEADS = 8
GQA_KV_HEADS = 2
GQA_HEAD_DIM = 64
SC_WIDTH = 512
MLA_HEADS = 8
MLA_Q_LORA = 384
MLA_KV_LORA = 256
MLA_NOPE = 64
MLA_ROPE = 32
MLA_V = 64
LRU_WIDTH = 512
LRU_C = 8.0
N_BRANCH = 4
LOG2E = 1.4426950408889634

VMEM_LIMIT_BYTES = 56 * 1024 * 1024
HALO = 8
QK_PAD = 128
NEG_BIG = -1e30
MAX_UNSHIFTED_LOGIT = 64.0


def _params(*sem):
    return pltpu.CompilerParams(dimension_semantics=sem, vmem_limit_bytes=VMEM_LIMIT_BYTES)


def _const_spec(shape):
    zeros = (0,) * len(shape)
    return pl.BlockSpec(shape, lambda *_: zeros, pipeline_mode=pl.Buffered(1))


def _rms_rows(x, g):
    return x * lax.rsqrt(jnp.mean(x * x, axis=-1, keepdims=True) + EPS) * g


def _dot(a, b):
    return jnp.dot(a, b, preferred_element_type=F32)


def _dot_nt(a, b):
    return lax.dot_general(a, b, (((1,), (1,)), ((), ())), preferred_element_type=F32)


def _dot_tn(a, b):
    return lax.dot_general(a, b, (((0,), (0,)), ((), ())), preferred_element_type=F32)


def _ffn_body(x_ref, g_ref, wg_ref, wu_ref, wo_ref, o_ref, *, n_chunks):
    x = x_ref[...]
    h = _rms_rows(x, g_ref[...]).astype(BF16)
    tf = wg_ref.shape[1] // n_chunks
    acc = jnp.zeros(x.shape, F32)
    for c in range(n_chunks):
        gate = _dot(h, wg_ref[:, c * tf:(c + 1) * tf])
        up = _dot(h, wu_ref[:, c * tf:(c + 1) * tf])
        a = (gate * jax.nn.sigmoid(gate) * up).astype(BF16)
        acc = acc + _dot(a, wo_ref[c * tf:(c + 1) * tf, :])
    o_ref[...] = x + 0.5 * acc


def _ffn(x2d, g, wi, wo, *, tm):
    t, d = x2d.shape
    d_ff = wo.shape[0]
    wg = wi[:, :d_ff].astype(BF16)
    wu = wi[:, d_ff:].astype(BF16)
    n_chunks = 2 if d_ff % 256 == 0 else 1
    return pl.pallas_call(
        functools.partial(_ffn_body, n_chunks=n_chunks),
        grid=(t // tm,),
        in_specs=[
            pl.BlockSpec((tm, d), lambda i: (i, 0)),
            _const_spec((1, d)),
            _const_spec((d, d_ff)),
            _const_spec((d, d_ff)),
            _const_spec((d_ff, d)),
        ],
        out_specs=pl.BlockSpec((tm, d), lambda i: (i, 0)),
        out_shape=jax.ShapeDtypeStruct((t, d), F32),
        compiler_params=_params("parallel"),
        name="ffn",
    )(x2d, g.reshape(1, d), wg, wu, wo.astype(BF16))


def _rope_tables(s, dim):
    half = dim // 2
    inv = (np.float32(ROPE_THETA) ** (-np.arange(0, half, 2, dtype=np.float32) / np.float32(half))).astype(np.float32)
    t = np.arange(s)
    row = (t // GRID_W).astype(np.float32)
    col = (t % GRID_W).astype(np.float32)
    ang = np.concatenate([inv[:, None] * row[None, :], inv[:, None] * col[None, :]], axis=0)
    ang = ang.astype(np.float32).astype(np.float64)
    return jnp.asarray(np.cos(ang), F32), jnp.asarray(np.sin(ang), F32)


def _axial_rope(t, cos, sin):
    q = t.shape[1] // 4
    x1r, x2r, x1c, x2c = t[:, 0:q], t[:, q:2 * q], t[:, 2 * q:3 * q], t[:, 3 * q:4 * q]
    cr, sr = cos[None, 0:q], sin[None, 0:q]
    cc, sc = cos[None, q:2 * q], sin[None, q:2 * q]
    return jnp.concatenate(
        [x1r * cr - x2r * sr, x2r * cr + x1r * sr, x1c * cc - x2c * sc, x2c * cc + x1c * sc], axis=1)


def _gqa_prep_body(x_ref, g_ref, wt_ref, qg_ref, kg_ref, cos_ref, sin_ref, q_out, k_out, v_out):
    x = x_ref[0]
    tm = x.shape[0]
    h = _rms_rows(x, g_ref[...]).astype(BF16)
    y = _dot_nt(wt_ref[...], h)
    cos = cos_ref[...]
    sin = sin_ref[...]
    nq = GQA_HEADS * GQA_HEAD_DIM
    nk = GQA_KV_HEADS * GQA_HEAD_DIM

    def norm_rope(t, gain):
        r = lax.rsqrt(jnp.mean(t * t, axis=1, keepdims=True) + EPS)
        return _axial_rope(t * r * gain[None], cos, sin)

    q = norm_rope(y[0:nq].reshape(GQA_HEADS, GQA_HEAD_DIM, tm), qg_ref[...])
    q = q * (GQA_HEAD_DIM ** -0.5 * LOG2E)
    grp = GQA_HEADS // GQA_KV_HEADS
    z = jnp.zeros((grp, GQA_HEAD_DIM, tm), F32)
    qp = jnp.concatenate(
        [jnp.concatenate([q[:grp], z], axis=1), jnp.concatenate([z, q[grp:]], axis=1)], axis=0)
    q_out[0] = qp.astype(BF16)
    k = norm_rope(y[nq:nq + nk].reshape(GQA_KV_HEADS, GQA_HEAD_DIM, tm), kg_ref[...])
    k_out[0, 0, 0] = k.reshape(nk, tm).T.astype(BF16)
    v_out[0, :, 0] = y[nq + nk:nq + 2 * nk].reshape(GQA_KV_HEADS, GQA_HEAD_DIM, tm).astype(BF16)


def _gqa_prep(x, g, w_qkv, qg, kg, cos, sin, *, tm):
    b, s, d = x.shape
    n = w_qkv.shape[1]
    return pl.pallas_call(
        _gqa_prep_body,
        grid=(b, s // tm),
        in_specs=[
            pl.BlockSpec((1, tm, d), lambda bi, i: (bi, i, 0)),
            _const_spec((1, d)),
            _const_spec((n, d)),
            _const_spec((GQA_HEAD_DIM, 1)),
            _const_spec((GQA_HEAD_DIM, 1)),
            pl.BlockSpec((GQA_HEAD_DIM // 2, tm), lambda bi, i: (0, i)),
            pl.BlockSpec((GQA_HEAD_DIM // 2, tm), lambda bi, i: (0, i)),
        ],
        out_specs=[
            pl.BlockSpec((1, GQA_HEADS, QK_PAD, tm), lambda bi, i: (bi, 0, 0, i)),
            pl.BlockSpec((1, 1, 1, tm, QK_PAD), lambda bi, i: (bi, 0, i, 0, 0)),
            pl.BlockSpec((1, GQA_KV_HEADS, 1, GQA_HEAD_DIM, tm), lambda bi, i: (bi, 0, i, 0, 0)),
        ],
        out_shape=[
            jax.ShapeDtypeStruct((b, GQA_HEADS, QK_PAD, s), BF16),
            jax.ShapeDtypeStruct((b, 1, s // tm, tm, QK_PAD), BF16),
            jax.ShapeDtypeStruct((b, GQA_KV_HEADS, s // tm, GQA_HEAD_DIM, tm), BF16),
        ],
        compiler_params=_params("parallel", "parallel"),
        name="gqa_prep",
    )(x, g.reshape(1, d), w_qkv.T.astype(BF16), qg.reshape(-1, 1), kg.reshape(-1, 1), cos, sin)


def _mla_prep_body(x_ref, g_ref, wlat_ref, wkr_ref, qan_ref, kvan_ref, wqt_ref, wkvt_ref,
                   qg_ref, kg_ref, cos_ref, sin_ref, q_out, k_out, v_out):
    x = x_ref[0]
    tm = x.shape[0]
    h = _rms_rows(x, g_ref[...]).astype(BF16)
    lat = _dot(h, wlat_ref[...])
    qn = _rms_rows(lat[:, :MLA_Q_LORA], qan_ref[...]).astype(BF16)
    kvn = _rms_rows(lat[:, MLA_Q_LORA:], kvan_ref[...]).astype(BF16)
    dqk = MLA_NOPE + MLA_ROPE
    q = _dot_nt(wqt_ref[...], qn).reshape(MLA_HEADS, dqk, tm)
    kv = _dot_nt(wkvt_ref[...], kvn).reshape(MLA_HEADS, MLA_NOPE + MLA_V, tm)
    kr = _dot_nt(wkr_ref[...], h)
    cos = cos_ref[...]
    sin = sin_ref[...]
    pad = jnp.zeros((MLA_HEADS, QK_PAD - dqk, tm), F32)

    qg = qg_ref[...]
    rq = lax.rsqrt(jnp.mean(q * q, axis=1, keepdims=True) + EPS)
    qs = q * rq * (qg[None] * (dqk ** -0.5 * LOG2E))
    qf = jnp.concatenate([qs[:, :MLA_NOPE], _axial_rope(qs[:, MLA_NOPE:], cos, sin), pad], axis=1)
    q_out[0] = qf.astype(BF16)

    kg = kg_ref[...]
    k_nope = kv[:, :MLA_NOPE]
    ss = jnp.sum(k_nope * k_nope, axis=1, keepdims=True) + jnp.sum(kr * kr, axis=0, keepdims=True)[None]
    rk = lax.rsqrt(ss * (1.0 / dqk) + EPS)
    kn = k_nope * rk * kg[None, :MLA_NOPE]
    krn = _axial_rope(kr[None] * rk * kg[None, MLA_NOPE:], cos, sin)
    kf = jnp.concatenate([kn, krn, pad], axis=1)
    for hh in range(MLA_HEADS):
        k_out[0, hh, 0] = kf[hh].T.astype(BF16)
    v_out[0, :, 0] = kv[:, MLA_NOPE:].astype(BF16)


def _mla_prep(x, g, w_lat, w_kr, qan, kvan, wq_up, wkv_up, qg, kg, cos, sin, *, tm):
    b, s, d = x.shape
    dqk = MLA_NOPE + MLA_ROPE
    return pl.pallas_call(
        _mla_prep_body,
        grid=(b, s // tm),
        in_specs=[
            pl.BlockSpec((1, tm, d), lambda bi, i: (bi, i, 0)),
            _const_spec((1, d)),
            _const_spec((d, MLA_Q_LORA + MLA_KV_LORA)),
            _const_spec((MLA_ROPE, d)),
            _const_spec((1, MLA_Q_LORA)),
            _const_spec((1, MLA_KV_LORA)),
            _const_spec((MLA_HEADS * dqk, MLA_Q_LORA)),
            _const_spec((MLA_HEADS * (MLA_NOPE + MLA_V), MLA_KV_LORA)),
            _const_spec((dqk, 1)),
            _const_spec((dqk, 1)),
            pl.BlockSpec((MLA_ROPE // 2, tm), lambda bi, i: (0, i)),
            pl.BlockSpec((MLA_ROPE // 2, tm), lambda bi, i: (0, i)),
        ],
        out_specs=[
            pl.BlockSpec((1, MLA_HEADS, QK_PAD, tm), lambda bi, i: (bi, 0, 0, i)),
            pl.BlockSpec((1, MLA_HEADS, 1, tm, QK_PAD), lambda bi, i: (bi, 0, i, 0, 0)),
            pl.BlockSpec((1, MLA_HEADS, 1, MLA_V, tm), lambda bi, i: (bi, 0, i, 0, 0)),
        ],
        out_shape=[
            jax.ShapeDtypeStruct((b, MLA_HEADS, QK_PAD, s), BF16),
            jax.ShapeDtypeStruct((b, MLA_HEADS, s // tm, tm, QK_PAD), BF16),
            jax.ShapeDtypeStruct((b, MLA_HEADS, s // tm, MLA_V, tm), BF16),
        ],
        compiler_params=_params("parallel", "parallel"),
        name="mla_prep",
    )(x, g.reshape(1, d), w_lat.astype(BF16), w_kr.T.astype(BF16), qan.reshape(1, -1),
      kvan.reshape(1, -1), wq_up.T.astype(BF16), wkv_up.T.astype(BF16),
      qg.reshape(-1, 1), kg.reshape(-1, 1), cos, sin)


def _attn_body(q_ref, k_ref, v_ref, o_ref, s0_scr, s1_scr, mx0_scr, mx1_scr, m_scr, l_scr, acc_scr,
               *, strip):
    tq = q_ref.shape[3]
    n_kt = k_ref.shape[2]
    strips = [slice(c * strip, (c + 1) * strip) for c in range(tq // strip)]
    slots = ((s0_scr, mx0_scr), (s1_scr, mx1_scr))
    m_scr[...] = jnp.full(m_scr.shape, NEG_BIG, F32)
    l_scr[...] = jnp.zeros(l_scr.shape, F32)
    acc_scr[...] = jnp.zeros(acc_scr.shape, F32)

    def scores(j, slot):
        s_scr, mx_scr = slots[slot]
        k = k_ref[0, 0, j]
        for sl in strips:
            s = _dot(k, q_ref[0, 0, :, sl])
            s_scr[:, sl] = s
            mx_scr[:, sl] = jnp.max(s, axis=0, keepdims=True)

    def accumulate(j, slot):
        s_scr, mx_scr = slots[slot]
        v = v_ref[0, 0, j]
        for sl in strips:
            m_prev = m_scr[:, sl]
            m_new = jnp.maximum(m_prev, mx_scr[:, sl])
            alpha = jnp.exp2(m_prev - m_new)
            p = jnp.exp2(s_scr[:, sl] - m_new)
            l_scr[:, sl] = alpha * l_scr[:, sl] + jnp.sum(p, axis=0, keepdims=True)
            acc_scr[:, sl] = alpha * acc_scr[:, sl] + _dot(v, p.astype(BF16))
            m_scr[:, sl] = m_new

    def pair(i, carry):
        j = 2 * i
        scores(j + 1, 1)
        accumulate(j, 0)
        scores(j + 2, 0)
        accumulate(j + 1, 1)
        return carry

    scores(0, 0)
    n_pairs = (n_kt - 1) // 2
    lax.fori_loop(0, n_pairs, pair, 0)
    if n_kt == 2 * n_pairs + 2:
        scores(n_kt - 1, 1)
        accumulate(n_kt - 2, 0)
        accumulate(n_kt - 1, 1)
    else:
        accumulate(n_kt - 1, 0)
    o_ref[0, 0] = (acc_scr[...] / l_scr[...]).astype(o_ref.dtype)


def _attn_bounded_body(q_ref, k_ref, v_ref, o_ref, p0_scr, p1_scr, l_scr, acc_scr, *, strip,
                       steps_per_iter):
    tq = q_ref.shape[3]
    n_kt, tk = k_ref.shape[2], k_ref.shape[3]
    strips = [slice(c * strip, (c + 1) * strip) for c in range(tq // strip)]
    p_slots = (p0_scr, p1_scr)
    l_scr[...] = jnp.zeros(l_scr.shape, F32)
    acc_scr[...] = jnp.zeros(acc_scr.shape, F32)

    def probs(j, slot, sl):
        p = jnp.exp2(_dot(k_ref[0, 0, j], q_ref[0, 0, :, sl]))
        l_scr[:, sl] += jnp.sum(p.reshape(tk // 8, 8, strip), axis=0)
        p_slots[slot][:, sl] = p.astype(BF16)

    def accumulate(j, slot, sl):
        acc_scr[:, sl] += _dot(v_ref[0, 0, j], p_slots[slot][:, sl])

    def step(j_probs, j_acc, slot_probs):
        for sl in strips:
            if j_probs is not None:
                probs(j_probs, slot_probs, sl)
            if j_acc is not None:
                accumulate(j_acc, 1 - slot_probs, sl)

    def group(i, carry):
        for u in range(steps_per_iter):
            j = i * steps_per_iter + u
            step(j + 1, j, (u + 1) % 2)
        return carry

    step(0, None, 0)
    n_iter = (n_kt - 1) // steps_per_iter
    lax.fori_loop(0, n_iter, group, 0)
    for j in range(n_iter * steps_per_iter, n_kt):
        step(j + 1 if j + 1 < n_kt else None, j, (j + 1) % 2)
    l = jnp.sum(l_scr[...], axis=0, keepdims=True)
    o_ref[0, 0] = (acc_scr[...] / l).astype(o_ref.dtype)


def _attention(qt, k, vt, logit_bound, *, tq, strip):
    b, nh, _, s = qt.shape
    _, nhk, n_kt, tk, _ = k.shape
    kdiv = nh // nhk
    vdiv = nh // vt.shape[1]
    dv = vt.shape[3]
    common = dict(
        grid=(b, nh, s // tq),
        in_specs=[
            pl.BlockSpec((1, 1, QK_PAD, tq), lambda bi, h, qi: (bi, h, 0, qi)),
            pl.BlockSpec((1, 1, n_kt, tk, QK_PAD), lambda bi, h, qi: (bi, h // kdiv, 0, 0, 0)),
            pl.BlockSpec((1, 1, n_kt, dv, tk), lambda bi, h, qi: (bi, h // vdiv, 0, 0, 0)),
        ],
        out_specs=pl.BlockSpec((1, 1, dv, tq), lambda bi, h, qi: (bi, h, 0, qi)),
        out_shape=jax.ShapeDtypeStruct((b, nh, dv, s), BF16),
        compiler_params=_params("parallel", "parallel", "parallel"),
    )
    online = pl.pallas_call(
        functools.partial(_attn_body, strip=strip),
        scratch_shapes=[
            pltpu.VMEM((tk, tq), F32),
            pltpu.VMEM((tk, tq), F32),
            pltpu.VMEM((1, tq), F32),
            pltpu.VMEM((1, tq), F32),
            pltpu.VMEM((1, tq), F32),
            pltpu.VMEM((1, tq), F32),
            pltpu.VMEM((dv, tq), F32),
        ],
        name="attention_online",
        **common,
    )
    bounded = pl.pallas_call(
        functools.partial(_attn_bounded_body, strip=strip, steps_per_iter=4),
        scratch_shapes=[
            pltpu.VMEM((tk, tq), BF16),
            pltpu.VMEM((tk, tq), BF16),
            pltpu.VMEM((8, tq), F32),
            pltpu.VMEM((dv, tq), F32),
        ],
        name="attention_bounded",
        **common,
    )
    return lax.cond(logit_bound <= MAX_UNSHIFTED_LOGIT, bounded, online, qt, k, vt)


def _logit_bound(q_gain, k_gain, head_dim):
    return (1.02 * LOG2E * head_dim ** 0.5) * jnp.max(jnp.abs(q_gain)) * jnp.max(jnp.abs(k_gain))


def _halo_specs(tm, d, n_tiles):
    nb = tm // HALO
    return [
        pl.BlockSpec((1, tm, d), lambda bi, i: (bi, i, 0)),
        pl.BlockSpec((1, HALO, d), lambda bi, i: (bi, jnp.maximum(i * nb - 1, 0), 0)),
        pl.BlockSpec((1, HALO, d), lambda bi, i: (bi, jnp.minimum((i + 1) * nb, n_tiles * nb - 1), 0)),
    ]


def _halo_rows_valid(tm, i, n_tiles):
    r = lax.broadcasted_iota(jnp.int32, (tm + 2 * HALO, 1), 0)
    lo = jnp.where(i == 0, HALO, 0)
    hi = jnp.where(i == n_tiles - 1, tm + HALO, tm + 2 * HALO)
    return ((r >= lo) & (r < hi)).astype(F32)


def _shortconv_body(xm_ref, xp_ref, xn_ref, g_ref, w_ref, cw_ref, cb_ref, o_ref, z_scr):
    i = pl.program_id(1)
    tm = xm_ref.shape[1]
    x = jnp.concatenate([xp_ref[0], xm_ref[0], xn_ref[0]], axis=0)
    h = _rms_rows(x, g_ref[...]).astype(BF16)
    u = _dot(h, w_ref[...])
    w = SC_WIDTH
    z_scr[...] = u[:, w:2 * w] * u[:, 2 * w:] * _halo_rows_valid(tm, i, pl.num_programs(1))
    cw = cw_ref[...]
    conv = cb_ref[...]
    for tap in range(cw.shape[0]):
        conv = conv + cw[tap:tap + 1] * z_scr[pl.ds(HALO - 1 + tap, tm), :]
    o_ref[0] = (u[HALO:HALO + tm, :w] * conv).astype(o_ref.dtype)


def _shortconv(x, g, w_sc, cw, cb, *, tm):
    b, s, d = x.shape
    return pl.pallas_call(
        _shortconv_body,
        grid=(b, s // tm),
        in_specs=_halo_specs(tm, d, s // tm) + [
            _const_spec((1, d)),
            _const_spec((d, 3 * SC_WIDTH)),
            _const_spec(cw.shape),
            _const_spec((1, SC_WIDTH)),
        ],
        out_specs=pl.BlockSpec((1, tm, SC_WIDTH), lambda bi, i: (bi, i, 0)),
        out_shape=jax.ShapeDtypeStruct((b, s, SC_WIDTH), BF16),
        scratch_shapes=[pltpu.VMEM((tm + 2 * HALO, SC_WIDTH), F32)],
        compiler_params=_params("parallel", "parallel"),
        name="shortconv",
    )(x, x, x, g.reshape(1, d), w_sc.astype(BF16), cw, cb.reshape(1, -1))


def _softplus(z):
    return jnp.maximum(z, 0.0) + jnp.log1p(jnp.exp(-jnp.abs(z)))


def _lru_gates(xm_ref, xp_ref, xn_ref, g_ref, w_ref, cw_ref, cb_ref, wa_ref, ba_ref, wx_ref, bx_ref,
               lam_ref, xb_scr, a_scr, b_scr, tile, n_tiles, d):
    tm = xm_ref.shape[1]
    x = jnp.concatenate([xp_ref[0], xm_ref[0], xn_ref[0]], axis=0)
    h = _rms_rows(x, g_ref[...]).astype(BF16)
    xb_scr[...] = _dot(h, w_ref[...]) * _halo_rows_valid(tm, tile, n_tiles)
    cw = cw_ref[...]
    xc = cb_ref[...]
    for tap in range(cw.shape[0]):
        xc = xc + cw[tap:tap + 1] * xb_scr[pl.ds(HALO - 2 + tap, tm), :]
    xcb = xc.astype(BF16)
    r = jax.nn.sigmoid(_dot(xcb, wa_ref[d]) + ba_ref[d:d + 1])
    ig = jax.nn.sigmoid(_dot(xcb, wx_ref[d]) + bx_ref[d:d + 1])
    log_a = (-LRU_C) * r * _softplus(-lam_ref[d:d + 1])
    a = jnp.exp(log_a)
    mult = jnp.sqrt(-jnp.tanh(log_a) * (a * a + 1.0))
    a_scr[...] = a
    b_scr[...] = mult * (ig * xc)


def _lru_body(xmf_ref, xpf_ref, xnf_ref, xmb_ref, xpb_ref, xnb_ref, g_ref, w_ref, cw_ref, cb_ref,
              wa_ref, ba_ref, wx_ref, bx_ref, lam_ref, hf_ref, hb_ref,
              xb_scr, af_scr, bf_scr, ab_scr, bb_scr, cf_scr, cbk_scr):
    i = pl.program_id(1)
    n_tiles = pl.num_programs(1)
    tm = xmf_ref.shape[1]

    @pl.when(i == 0)
    def _():
        cf_scr[...] = jnp.zeros(cf_scr.shape, F32)
        cbk_scr[...] = jnp.zeros(cbk_scr.shape, F32)

    shared = (g_ref, w_ref, cw_ref, cb_ref, wa_ref, ba_ref, wx_ref, bx_ref, lam_ref, xb_scr)
    _lru_gates(xmf_ref, xpf_ref, xnf_ref, *shared, af_scr, bf_scr, i, n_tiles, 0)
    _lru_gates(xmb_ref, xpb_ref, xnb_ref, *shared, ab_scr, bb_scr, n_tiles - 1 - i, n_tiles, 1)

    row = lax.broadcasted_iota(jnp.int32, (HALO, LRU_WIDTH), 0)
    nblk = tm // HALO

    def block(j, carry):
        cf, cbk = carry
        jf = pl.multiple_of(j * HALO, HALO)
        jb = pl.multiple_of((nblk - 1 - j) * HALO, HALO)
        a, bv = af_scr[pl.ds(jf, HALO), :], bf_scr[pl.ds(jf, HALO), :]
        a2, bv2 = ab_scr[pl.ds(jb, HALO), :], bb_scr[pl.ds(jb, HALO), :]
        for sh in (1, 2, 4):
            keep = row >= sh
            a_s = jnp.where(keep, pltpu.roll(a, sh, 0), 1.0)
            b_s = jnp.where(keep, pltpu.roll(bv, sh, 0), 0.0)
            bv = a * b_s + bv
            a = a * a_s
            keep2 = row < HALO - sh
            a2_s = jnp.where(keep2, pltpu.roll(a2, HALO - sh, 0), 1.0)
            b2_s = jnp.where(keep2, pltpu.roll(bv2, HALO - sh, 0), 0.0)
            bv2 = a2 * b2_s + bv2
            a2 = a2 * a2_s
        hf = a * cf + bv
        hb = a2 * cbk + bv2
        hf_ref[0, pl.ds(jf, HALO), :] = hf
        hb_ref[0, pl.ds(jb, HALO), :] = hb
        return hf[HALO - 1:HALO], hb[0:1]

    cf, cbk = lax.fori_loop(0, nblk, block, (cf_scr[...], cbk_scr[...]))
    cf_scr[...] = cf
    cbk_scr[...] = cbk


def _lru(x, g, w_x, cw, cb, wa_bd, ba, wx_bd, bx, lam, *, tm):
    b, s, d = x.shape
    n = s // tm
    nb = tm // HALO
    w = LRU_WIDTH
    fwd = _halo_specs(tm, d, n)
    bwd = [
        pl.BlockSpec((1, tm, d), lambda bi, i: (bi, n - 1 - i, 0)),
        pl.BlockSpec((1, HALO, d), lambda bi, i: (bi, jnp.maximum((n - 1 - i) * nb - 1, 0), 0)),
        pl.BlockSpec((1, HALO, d), lambda bi, i: (bi, jnp.minimum((n - i) * nb, n * nb - 1), 0)),
    ]
    return pl.pallas_call(
        _lru_body,
        grid=(b, n),
        in_specs=fwd + bwd + [
            _const_spec((1, d)),
            _const_spec((d, w)),
            _const_spec(cw.shape),
            _const_spec((1, w)),
            _const_spec((2, w, w)),
            _const_spec((2, w)),
            _const_spec((2, w, w)),
            _const_spec((2, w)),
            _const_spec((2, w)),
        ],
        out_specs=[
            pl.BlockSpec((1, tm, w), lambda bi, i: (bi, i, 0)),
            pl.BlockSpec((1, tm, w), lambda bi, i: (bi, n - 1 - i, 0)),
        ],
        out_shape=[jax.ShapeDtypeStruct((b, s, w), F32)] * 2,
        scratch_shapes=[pltpu.VMEM((tm + 2 * HALO, w), F32)]
        + [pltpu.VMEM((tm, w), F32)] * 4
        + [pltpu.VMEM((1, w), F32)] * 2,
        compiler_params=_params("parallel", "arbitrary"),
        name="rglru",
    )(x, x, x, x, x, x, g.reshape(1, d), w_x.astype(BF16), cw, cb.reshape(1, -1),
      wa_bd.astype(BF16), ba, wx_bd.astype(BF16), bx, lam)


def _block_diag(wblk):
    two, n, c, _ = wblk.shape
    eye = jnp.eye(n, dtype=wblk.dtype)
    return jnp.einsum("dnij,nm->dnimj", wblk, eye).reshape(two, n * c, n * c)


def _gelu_tanh(x):
    return 0.5 * x * (1.0 + jnp.tanh(0.7978845608028654 * (x + 0.044715 * (x * x * x))))


def _merge_body(x_ref, g_ref, oa_ref, ob_ref, oc_ref, hf_ref, hb_ref, wgate_ref, wlg_ref, wbr_ref,
                wout_ref, o_ref):
    x = x_ref[0]
    d = x.shape[1]
    h = _rms_rows(x, g_ref[...]).astype(BF16)
    od = (_gelu_tanh(_dot(h, wlg_ref[...])) * (hf_ref[0] + hb_ref[0])).astype(BF16)
    ys = (
        _dot_tn(oa_ref[0], wbr_ref[0]),
        _dot(ob_ref[0], wbr_ref[1]),
        _dot_tn(oc_ref[0], wbr_ref[2]),
        _dot(od, wbr_ref[3]),
    )
    merged = jnp.zeros(x.shape, F32)
    for n, y in enumerate(ys):
        gate = jax.nn.sigmoid(_dot(h, wgate_ref[:, n * d:(n + 1) * d]))
        merged = merged + gate * y
    o_ref[0] = x + _dot(merged.astype(BF16), wout_ref[...])


def _merge(x, g, oa_t, ob, oc_t, hf, hb, w_gate, w_lg, w_branch, w_out, *, tm):
    b, s, d = x.shape
    w = oa_t.shape[1]
    row = lambda width: pl.BlockSpec((1, tm, width), lambda bi, i: (bi, i, 0))
    col = pl.BlockSpec((1, w, tm), lambda bi, i: (bi, 0, i))
    return pl.pallas_call(
        _merge_body,
        grid=(b, s // tm),
        in_specs=[
            row(d), _const_spec((1, d)), col, row(w), col, row(w), row(w),
            _const_spec((d, N_BRANCH * d)),
            _const_spec((d, w)),
            _const_spec((N_BRANCH, w, d)),
            _const_spec((d, d)),
        ],
        out_specs=row(d),
        out_shape=jax.ShapeDtypeStruct((b, s, d), F32),
        compiler_params=_params("parallel", "parallel"),
        name="merge",
    )(x, g.reshape(1, d), oa_t, ob, oc_t, hf, hb, w_gate.astype(BF16), w_lg.astype(BF16),
      w_branch.astype(BF16), w_out.astype(BF16))


def _row_tile(n, target):
    t = min(n, target)
    while n % t:
        t //= 2
    return t


def kernel(x, ffn1_norm, ffn1_wi, ffn1_wo, mix_norm, w_in, gqa_q_norm, gqa_k_norm, sc_conv_w, sc_conv_b, mla_qa_norm, mla_wq_up, mla_kva_norm, mla_wkv_up, mla_q_norm, mla_k_norm, lru_conv_w, lru_conv_b, lru_wa, lru_ba, lru_wx, lru_bx, lru_lambda, w_branch, w_out, ffn2_norm, ffn2_wi, ffn2_wo):
    b, s, d = x.shape
    depth = ffn1_wi.shape[0]
    tm = _row_tile(s, 512)
    tq = _row_tile(s, 2048)
    strip = min(tq, 256)
    gqa_cos, gqa_sin = _rope_tables(s, GQA_HEAD_DIM)
    mla_cos, mla_sin = _rope_tables(s, MLA_ROPE)

    n_q = GQA_HEADS * GQA_HEAD_DIM
    n_kv = GQA_KV_HEADS * GQA_HEAD_DIM
    o_sc = n_q + 2 * n_kv
    o_qlat = o_sc + 3 * SC_WIDTH
    o_kr = o_qlat + MLA_Q_LORA + MLA_KV_LORA
    o_lru = o_kr + MLA_ROPE
    o_gate = o_lru + 2 * LRU_WIDTH

    for l in range(depth):
        x = _ffn(x.reshape(b * s, d), ffn1_norm[l], ffn1_wi[l], ffn1_wo[l], tm=tm).reshape(b, s, d)

        wl = w_in[l]
        g = mix_norm[l]
        qt, kk, vt = _gqa_prep(x, g, wl[:, :o_sc], gqa_q_norm[l], gqa_k_norm[l], gqa_cos, gqa_sin, tm=tm)
        oa_t = _attention(qt, kk, vt, _logit_bound(gqa_q_norm[l], gqa_k_norm[l], GQA_HEAD_DIM),
                          tq=tq, strip=strip).reshape(b, GQA_HEADS * GQA_HEAD_DIM, s)
        qt, kk, vt = _mla_prep(x, g, wl[:, o_qlat:o_kr], wl[:, o_kr:o_lru], mla_qa_norm[l],
                               mla_kva_norm[l], mla_wq_up[l], mla_wkv_up[l], mla_q_norm[l],
                               mla_k_norm[l], mla_cos, mla_sin, tm=tm)
        oc_t = _attention(qt, kk, vt, _logit_bound(mla_q_norm[l], mla_k_norm[l], MLA_NOPE + MLA_ROPE),
                          tq=tq, strip=strip).reshape(b, MLA_HEADS * MLA_V, s)
        ob = _shortconv(x, g, wl[:, o_sc:o_qlat], sc_conv_w[l], sc_conv_b[l], tm=tm)
        hf, hb = _lru(x, g, wl[:, o_lru + LRU_WIDTH:o_gate], lru_conv_w[l], lru_conv_b[l],
                      _block_diag(lru_wa[l]), lru_ba[l], _block_diag(lru_wx[l]), lru_bx[l],
                      lru_lambda[l], tm=tm)
        x = _merge(x, g, oa_t, ob, oc_t, hf, hb, wl[:, o_gate:], wl[:, o_lru:o_lru + LRU_WIDTH],
                   w_branch[l], w_out[l], tm=tm)

        x = _ffn(x.reshape(b * s, d), ffn2_norm[l], ffn2_wi[l], ffn2_wo[l], tm=tm).reshape(b, s, d)
    return x
```

```python
import functools

import numpy as np
import jax
import jax.numpy as jnp
from jax import lax
from jax.experimental import pallas as pl
from jax.experimental.pallas import tpu as pltpu

F32 = jnp.float32
BF16 = jnp.bfloat16

GRID_W = 64
ROPE_THETA = 10000.0
EPS = 1e-6
GQA_HEADS = 8
GQA_KV_HEADS = 2
GQA_HEAD_DIM = 64
SC_WIDTH = 512
MLA_HEADS = 8
MLA_Q_LORA = 384
MLA_KV_LORA = 256
MLA_NOPE = 64
MLA_ROPE = 32
MLA_V = 64
LRU_WIDTH = 512
LRU_C = 8.0
N_BRANCH = 4
LOG2E = 1.4426950408889634

VMEM_LIMIT_BYTES = 56 * 1024 * 1024
HALO = 8
QK_PAD = 128
NEG_BIG = -1e30
MAX_UNSHIFTED_LOGIT = 0.0


def _params(*sem):
    return pltpu.CompilerParams(dimension_semantics=sem, vmem_limit_bytes=VMEM_LIMIT_BYTES)


def _const_spec(shape):
    zeros = (0,) * len(shape)
    return pl.BlockSpec(shape, lambda *_: zeros, pipeline_mode=pl.Buffered(1))


def _rms_rows(x, g):
    return x * lax.rsqrt(jnp.mean(x * x, axis=-1, keepdims=True) + EPS) * g


def _dot(a, b):
    return jnp.dot(a, b, preferred_element_type=F32)


def _dot_nt(a, b):
    return lax.dot_general(a, b, (((1,), (1,)), ((), ())), preferred_element_type=F32)


def _dot_tn(a, b):
    return lax.dot_general(a, b, (((0,), (0,)), ((), ())), preferred_element_type=F32)


def _ffn_body(x_ref, g_ref, wg_ref, wu_ref, wo_ref, o_ref, *, n_chunks):
    x = x_ref[...]
    h = _rms_rows(x, g_ref[...]).astype(BF16)
    tf = wg_ref.shape[1] // n_chunks
    acc = jnp.zeros(x.shape, F32)
    for c in range(n_chunks):
        gate = _dot(h, wg_ref[:, c * tf:(c + 1) * tf])
        up = _dot(h, wu_ref[:, c * tf:(c + 1) * tf])
        a = (gate * jax.nn.sigmoid(gate) * up).astype(BF16)
        acc = acc + _dot(a, wo_ref[c * tf:(c + 1) * tf, :])
    o_ref[...] = x + 0.5 * acc


def _ffn(x2d, g, wi, wo, *, tm):
    t, d = x2d.shape
    d_ff = wo.shape[0]
    wg = wi[:, :d_ff].astype(BF16)
    wu = wi[:, d_ff:].astype(BF16)
    n_chunks = 2 if d_ff % 256 == 0 else 1
    return pl.pallas_call(
        functools.partial(_ffn_body, n_chunks=n_chunks),
        grid=(t // tm,),
        in_specs=[
            pl.BlockSpec((tm, d), lambda i: (i, 0)),
            _const_spec((1, d)),
            _const_spec((d, d_ff)),
            _const_spec((d, d_ff)),
            _const_spec((d_ff, d)),
        ],
        out_specs=pl.BlockSpec((tm, d), lambda i: (i, 0)),
        out_shape=jax.ShapeDtypeStruct((t, d), F32),
        compiler_params=_params("parallel"),
        name="ffn",
    )(x2d, g.reshape(1, d), wg, wu, wo.astype(BF16))


def _rope_tables(s, dim):
    half = dim // 2
    inv = (np.float32(ROPE_THETA) ** (-np.arange(0, half, 2, dtype=np.float32) / np.float32(half))).astype(np.float32)
    t = np.arange(s)
    row = (t // GRID_W).astype(np.float32)
    col = (t % GRID_W).astype(np.float32)
    ang = np.concatenate([inv[:, None] * row[None, :], inv[:, None] * col[None, :]], axis=0)
    ang = ang.astype(np.float32).astype(np.float64)
    return jnp.asarray(np.cos(ang), F32), jnp.asarray(np.sin(ang), F32)


def _axial_rope(t, cos, sin):
    q = t.shape[1] // 4
    x1r, x2r, x1c, x2c = t[:, 0:q], t[:, q:2 * q], t[:, 2 * q:3 * q], t[:, 3 * q:4 * q]
    cr, sr = cos[None, 0:q], sin[None, 0:q]
    cc, sc = cos[None, q:2 * q], sin[None, q:2 * q]
    return jnp.concatenate(
        [x1r * cr - x2r * sr, x2r * cr + x1r * sr, x1c * cc - x2c * sc, x2c * cc + x1c * sc], axis=1)


def _gqa_prep_body(x_ref, g_ref, wt_ref, qg_ref, kg_ref, cos_ref, sin_ref, q_out, k_out, v_out):
    x = x_ref[0]
    tm = x.shape[0]
    h = _rms_rows(x, g_ref[...]).astype(BF16)
    y = _dot_nt(wt_ref[...], h)
    cos = cos_ref[...]
    sin = sin_ref[...]
    nq = GQA_HEADS * GQA_HEAD_DIM
    nk = GQA_KV_HEADS * GQA_HEAD_DIM

    def norm_rope(t, gain):
        r = lax.rsqrt(jnp.mean(t * t, axis=1, keepdims=True) + EPS)
        return _axial_rope(t * r * gain[None], cos, sin)

    q = norm_rope(y[0:nq].reshape(GQA_HEADS, GQA_HEAD_DIM, tm), qg_ref[...])
    q = q * (GQA_HEAD_DIM ** -0.5 * LOG2E)
    grp = GQA_HEADS // GQA_KV_HEADS
    z = jnp.zeros((grp, GQA_HEAD_DIM, tm), F32)
    qp = jnp.concatenate(
        [jnp.concatenate([q[:grp], z], axis=1), jnp.concatenate([z, q[grp:]], axis=1)], axis=0)
    q_out[0] = qp.astype(BF16)
    k = norm_rope(y[nq:nq + nk].reshape(GQA_KV_HEADS, GQA_HEAD_DIM, tm), kg_ref[...])
    k_out[0, 0, 0] = k.reshape(nk, tm).T.astype(BF16)
    v_out[0, :, 0] = y[nq + nk:nq + 2 * nk].reshape(GQA_KV_HEADS, GQA_HEAD_DIM, tm).astype(BF16)


def _gqa_prep(x, g, w_qkv, qg, kg, cos, sin, *, tm):
    b, s, d = x.shape
    n = w_qkv.shape[1]
    return pl.pallas_call(
        _gqa_prep_body,
        grid=(b, s // tm),
        in_specs=[
            pl.BlockSpec((1, tm, d), lambda bi, i: (bi, i, 0)),
            _const_spec((1, d)),
            _const_spec((n, d)),
            _const_spec((GQA_HEAD_DIM, 1)),
            _const_spec((GQA_HEAD_DIM, 1)),
            pl.BlockSpec((GQA_HEAD_DIM // 2, tm), lambda bi, i: (0, i)),
            pl.BlockSpec((GQA_HEAD_DIM // 2, tm), lambda bi, i: (0, i)),
        ],
        out_specs=[
            pl.BlockSpec((1, GQA_HEADS, QK_PAD, tm), lambda bi, i: (bi, 0, 0, i)),
            pl.BlockSpec((1, 1, 1, tm, QK_PAD), lambda bi, i: (bi, 0, i, 0, 0)),
            pl.BlockSpec((1, GQA_KV_HEADS, 1, GQA_HEAD_DIM, tm), lambda bi, i: (bi, 0, i, 0, 0)),
        ],
        out_shape=[
            jax.ShapeDtypeStruct((b, GQA_HEADS, QK_PAD, s), BF16),
            jax.ShapeDtypeStruct((b, 1, s // tm, tm, QK_PAD), BF16),
            jax.ShapeDtypeStruct((b, GQA_KV_HEADS, s // tm, GQA_HEAD_DIM, tm), BF16),
        ],
        compiler_params=_params("parallel", "parallel"),
        name="gqa_prep",
    )(x, g.reshape(1, d), w_qkv.T.astype(BF16), qg.reshape(-1, 1), kg.reshape(-1, 1), cos, sin)


def _mla_prep_body(x_ref, g_ref, wlat_ref, wkr_ref, qan_ref, kvan_ref, wqt_ref, wkvt_ref,
                   qg_ref, kg_ref, cos_ref, sin_ref, q_out, k_out, v_out):
    x = x_ref[0]
    tm = x.shape[0]
    h = _rms_rows(x, g_ref[...]).astype(BF16)
    lat = _dot(h, wlat_ref[...])
    qn = _rms_rows(lat[:, :MLA_Q_LORA], qan_ref[...]).astype(BF16)
    kvn = _rms_rows(lat[:, MLA_Q_LORA:], kvan_ref[...]).astype(BF16)
    dqk = MLA_NOPE + MLA_ROPE
    q = _dot_nt(wqt_ref[...], qn).reshape(MLA_HEADS, dqk, tm)
    kv = _dot_nt(wkvt_ref[...], kvn).reshape(MLA_HEADS, MLA_NOPE + MLA_V, tm)
    kr = _dot_nt(wkr_ref[...], h)
    cos = cos_ref[...]
    sin = sin_ref[...]
    pad = jnp.zeros((MLA_HEADS, QK_PAD - dqk, tm), F32)

    qg = qg_ref[...]
    rq = lax.rsqrt(jnp.mean(q * q, axis=1, keepdims=True) + EPS)
    qs = q * rq * (qg[None] * (dqk ** -0.5 * LOG2E))
    qf = jnp.concatenate([qs[:, :MLA_NOPE], _axial_rope(qs[:, MLA_NOPE:], cos, sin), pad], axis=1)
    q_out[0] = qf.astype(BF16)

    kg = kg_ref[...]
    k_nope = kv[:, :MLA_NOPE]
    ss = jnp.sum(k_nope * k_nope, axis=1, keepdims=True) + jnp.sum(kr * kr, axis=0, keepdims=True)[None]
    rk = lax.rsqrt(ss * (1.0 / dqk) + EPS)
    kn = k_nope * rk * kg[None, :MLA_NOPE]
    krn = _axial_rope(kr[None] * rk * kg[None, MLA_NOPE:], cos, sin)
    kf = jnp.concatenate([kn, krn, pad], axis=1)
    for hh in range(MLA_HEADS):
        k_out[0, hh, 0] = kf[hh].T.astype(BF16)
    v_out[0, :, 0] = kv[:, MLA_NOPE:].astype(BF16)


def _mla_prep(x, g, w_lat, w_kr, qan, kvan, wq_up, wkv_up, qg, kg, cos, sin, *, tm):
    b, s, d = x.shape
    dqk = MLA_NOPE + MLA_ROPE
    return pl.pallas_call(
        _mla_prep_body,
        grid=(b, s // tm),
        in_specs=[
            pl.BlockSpec((1, tm, d), lambda bi, i: (bi, i, 0)),
            _const_spec((1, d)),
            _const_spec((d, MLA_Q_LORA + MLA_KV_LORA)),
            _const_spec((MLA_ROPE, d)),
            _const_spec((1, MLA_Q_LORA)),
            _const_spec((1, MLA_KV_LORA)),
            _const_spec((MLA_HEADS * dqk, MLA_Q_LORA)),
            _const_spec((MLA_HEADS * (MLA_NOPE + MLA_V), MLA_KV_LORA)),
            _const_spec((dqk, 1)),
            _const_spec((dqk, 1)),
            pl.BlockSpec((MLA_ROPE // 2, tm), lambda bi, i: (0, i)),
            pl.BlockSpec((MLA_ROPE // 2, tm), lambda bi, i: (0, i)),
        ],
        out_specs=[
            pl.BlockSpec((1, MLA_HEADS, QK_PAD, tm), lambda bi, i: (bi, 0, 0, i)),
            pl.BlockSpec((1, MLA_HEADS, 1, tm, QK_PAD), lambda bi, i: (bi, 0, i, 0, 0)),
            pl.BlockSpec((1, MLA_HEADS, 1, MLA_V, tm), lambda bi, i: (bi, 0, i, 0, 0)),
        ],
        out_shape=[
            jax.ShapeDtypeStruct((b, MLA_HEADS, QK_PAD, s), BF16),
            jax.ShapeDtypeStruct((b, MLA_HEADS, s // tm, tm, QK_PAD), BF16),
            jax.ShapeDtypeStruct((b, MLA_HEADS, s // tm, MLA_V, tm), BF16),
        ],
        compiler_params=_params("parallel", "parallel"),
        name="mla_prep",
    )(x, g.reshape(1, d), w_lat.astype(BF16), w_kr.T.astype(BF16), qan.reshape(1, -1),
      kvan.reshape(1, -1), wq_up.T.astype(BF16), wkv_up.T.astype(BF16),
      qg.reshape(-1, 1), kg.reshape(-1, 1), cos, sin)


def _attn_body(q_ref, k_ref, v_ref, o_ref, s0_scr, s1_scr, mx0_scr, mx1_scr, m_scr, l_scr, acc_scr,
               *, strip):
    tq = q_ref.shape[3]
    n_kt = k_ref.shape[2]
    strips = [slice(c * strip, (c + 1) * strip) for c in range(tq // strip)]
    slots = ((s0_scr, mx0_scr), (s1_scr, mx1_scr))
    m_scr[...] = jnp.full(m_scr.shape, NEG_BIG, F32)
    l_scr[...] = jnp.zeros(l_scr.shape, F32)
    acc_scr[...] = jnp.zeros(acc_scr.shape, F32)

    def scores(j, slot):
        s_scr, mx_scr = slots[slot]
        k = k_ref[0, 0, j]
        for sl in strips:
            s = _dot(k, q_ref[0, 0, :, sl])
            s_scr[:, sl] = s
            mx_scr[:, sl] = jnp.max(s, axis=0, keepdims=True)

    def accumulate(j, slot):
        s_scr, mx_scr = slots[slot]
        v = v_ref[0, 0, j]
        for sl in strips:
            m_prev = m_scr[:, sl]
            m_new = jnp.maximum(m_prev, mx_scr[:, sl])
            alpha = jnp.exp2(m_prev - m_new)
            p = jnp.exp2(s_scr[:, sl] - m_new)
            l_scr[:, sl] = alpha * l_scr[:, sl] + jnp.sum(p, axis=0, keepdims=True)
            acc_scr[:, sl] = alpha * acc_scr[:, sl] + _dot(v, p.astype(BF16))
            m_scr[:, sl] = m_new

    def pair(i, carry):
        j = 2 * i
        scores(j + 1, 1)
        accumulate(j, 0)
        scores(j + 2, 0)
        accumulate(j + 1, 1)
        return carry

    scores(0, 0)
    n_pairs = (n_kt - 1) // 2
    lax.fori_loop(0, n_pairs, pair, 0)
    if n_kt == 2 * n_pairs + 2:
        scores(n_kt - 1, 1)
        accumulate(n_kt - 2, 0)
        accumulate(n_kt - 1, 1)
    else:
        accumulate(n_kt - 1, 0)
    o_ref[0, 0] = (acc_scr[...] / l_scr[...]).astype(o_ref.dtype)


def _attn_bounded_body(q_ref, k_ref, v_ref, o_ref, p0_scr, p1_scr, l_scr, acc_scr, *, strip,
                       steps_per_iter):
    tq = q_ref.shape[3]
    n_kt, tk = k_ref.shape[2], k_ref.shape[3]
    strips = [slice(c * strip, (c + 1) * strip) for c in range(tq // strip)]
    p_slots = (p0_scr, p1_scr)
    l_scr[...] = jnp.zeros(l_scr.shape, F32)
    acc_scr[...] = jnp.zeros(acc_scr.shape, F32)

    def probs(j, slot, sl):
        p = jnp.exp2(_dot(k_ref[0, 0, j], q_ref[0, 0, :, sl]))
        l_scr[:, sl] += jnp.sum(p.reshape(tk // 8, 8, strip), axis=0)
        p_slots[slot][:, sl] = p.astype(BF16)

    def accumulate(j, slot, sl):
        acc_scr[:, sl] += _dot(v_ref[0, 0, j], p_slots[slot][:, sl])

    def step(j_probs, j_acc, slot_probs):
        for sl in strips:
            if j_probs is not None:
                probs(j_probs, slot_probs, sl)
            if j_acc is not None:
                accumulate(j_acc, 1 - slot_probs, sl)

    def group(i, carry):
        for u in range(steps_per_iter):
            j = i * steps_per_iter + u
            step(j + 1, j, (u + 1) % 2)
        return carry

    step(0, None, 0)
    n_iter = (n_kt - 1) // steps_per_iter
    lax.fori_loop(0, n_iter, group, 0)
    for j in range(n_iter * steps_per_iter, n_kt):
        step(j + 1 if j + 1 < n_kt else None, j, (j + 1) % 2)
    l = jnp.sum(l_scr[...], axis=0, keepdims=True)
    o_ref[0, 0] = (acc_scr[...] / l).astype(o_ref.dtype)


def _attention(qt, k, vt, logit_bound, *, tq, strip):
    b, nh, _, s = qt.shape
    _, nhk, n_kt, tk, _ = k.shape
    kdiv = nh // nhk
    vdiv = nh // vt.shape[1]
    dv = vt.shape[3]
    common = dict(
        grid=(b, nh, s // tq),
        in_specs=[
            pl.BlockSpec((1, 1, QK_PAD, tq), lambda bi, h, qi: (bi, h, 0, qi)),
            pl.BlockSpec((1, 1, n_kt, tk, QK_PAD), lambda bi, h, qi: (bi, h // kdiv, 0, 0, 0)),
            pl.BlockSpec((1, 1, n_kt, dv, tk), lambda bi, h, qi: (bi, h // vdiv, 0, 0, 0)),
        ],
        out_specs=pl.BlockSpec((1, 1, dv, tq), lambda bi, h, qi: (bi, h, 0, qi)),
        out_shape=jax.ShapeDtypeStruct((b, nh, dv, s), BF16),
        compiler_params=_params("parallel", "parallel", "parallel"),
    )
    online = pl.pallas_call(
        functools.partial(_attn_body, strip=strip),
        scratch_shapes=[
            pltpu.VMEM((tk, tq), F32),
            pltpu.VMEM((tk, tq), F32),
            pltpu.VMEM((1, tq), F32),
            pltpu.VMEM((1, tq), F32),
            pltpu.VMEM((1, tq), F32),
            pltpu.VMEM((1, tq), F32),
            pltpu.VMEM((dv, tq), F32),
        ],
        name="attention_online",
        **common,
    )
    bounded = pl.pallas_call(
        functools.partial(_attn_bounded_body, strip=strip, steps_per_iter=4),
        scratch_shapes=[
            pltpu.VMEM((tk, tq), BF16),
            pltpu.VMEM((tk, tq), BF16),
            pltpu.VMEM((8, tq), F32),
            pltpu.VMEM((dv, tq), F32),
        ],
        name="attention_bounded",
        **common,
    )
    return lax.cond(logit_bound <= MAX_UNSHIFTED_LOGIT, bounded, online, qt, k, vt)


def _logit_bound(q_gain, k_gain, head_dim):
    return (1.02 * LOG2E * head_dim ** 0.5) * jnp.max(jnp.abs(q_gain)) * jnp.max(jnp.abs(k_gain))


def _halo_specs(tm, d, n_tiles):
    nb = tm // HALO
    return [
        pl.BlockSpec((1, tm, d), lambda bi, i: (bi, i, 0)),
        pl.BlockSpec((1, HALO, d), lambda bi, i: (bi, jnp.maximum(i * nb - 1, 0), 0)),
        pl.BlockSpec((1, HALO, d), lambda bi, i: (bi, jnp.minimum((i + 1) * nb, n_tiles * nb - 1), 0)),
    ]


def _halo_rows_valid(tm, i, n_tiles):
    r = lax.broadcasted_iota(jnp.int32, (tm + 2 * HALO, 1), 0)
    lo = jnp.where(i == 0, HALO, 0)
    hi = jnp.where(i == n_tiles - 1, tm + HALO, tm + 2 * HALO)
    return ((r >= lo) & (r < hi)).astype(F32)


def _shortconv_body(xm_ref, xp_ref, xn_ref, g_ref, w_ref, cw_ref, cb_ref, o_ref, z_scr):
    i = pl.program_id(1)
    tm = xm_ref.shape[1]
    x = jnp.concatenate([xp_ref[0], xm_ref[0], xn_ref[0]], axis=0)
    h = _rms_rows(x, g_ref[...]).astype(BF16)
    u = _dot(h, w_ref[...])
    w = SC_WIDTH
    z_scr[...] = u[:, w:2 * w] * u[:, 2 * w:] * _halo_rows_valid(tm, i, pl.num_programs(1))
    cw = cw_ref[...]
    conv = cb_ref[...]
    for tap in range(cw.shape[0]):
        conv = conv + cw[tap:tap + 1] * z_scr[pl.ds(HALO - 1 + tap, tm), :]
    o_ref[0] = (u[HALO:HALO + tm, :w] * conv).astype(o_ref.dtype)


def _shortconv(x, g, w_sc, cw, cb, *, tm):
    b, s, d = x.shape
    return pl.pallas_call(
        _shortconv_body,
        grid=(b, s // tm),
        in_specs=_halo_specs(tm, d, s // tm) + [
            _const_spec((1, d)),
            _const_spec((d, 3 * SC_WIDTH)),
            _const_spec(cw.shape),
            _const_spec((1, SC_WIDTH)),
        ],
        out_specs=pl.BlockSpec((1, tm, SC_WIDTH), lambda bi, i: (bi, i, 0)),
        out_shape=jax.ShapeDtypeStruct((b, s, SC_WIDTH), BF16),
        scratch_shapes=[pltpu.VMEM((tm + 2 * HALO, SC_WIDTH), F32)],
        compiler_params=_params("parallel", "parallel"),
        name="shortconv",
    )(x, x, x, g.reshape(1, d), w_sc.astype(BF16), cw, cb.reshape(1, -1))


def _softplus(z):
    return jnp.maximum(z, 0.0) + jnp.log1p(jnp.exp(-jnp.abs(z)))


def _lru_gates(xm_ref, xp_ref, xn_ref, g_ref, w_ref, cw_ref, cb_ref, wa_ref, ba_ref, wx_ref, bx_ref,
               lam_ref, xb_scr, a_scr, b_scr, tile, n_tiles, d):
    tm = xm_ref.shape[1]
    x = jnp.concatenate([xp_ref[0], xm_ref[0], xn_ref[0]], axis=0)
    h = _rms_rows(x, g_ref[...]).astype(BF16)
    xb_scr[...] = _dot(h, w_ref[...]) * _halo_rows_valid(tm, tile, n_tiles)
    cw = cw_ref[...]
    xc = cb_ref[...]
    for tap in range(cw.shape[0]):
        xc = xc + cw[tap:tap + 1] * xb_scr[pl.ds(HALO - 2 + tap, tm), :]
    xcb = xc.astype(BF16)
    r = jax.nn.sigmoid(_dot(xcb, wa_ref[d]) + ba_ref[d:d + 1])
    ig = jax.nn.sigmoid(_dot(xcb, wx_ref[d]) + bx_ref[d:d + 1])
    log_a = (-LRU_C) * r * _softplus(-lam_ref[d:d + 1])
    a = jnp.exp(log_a)
    mult = jnp.sqrt(-jnp.tanh(log_a) * (a * a + 1.0))
    a_scr[...] = a
    b_scr[...] = mult * (ig * xc)


def _lru_body(xmf_ref, xpf_ref, xnf_ref, xmb_ref, xpb_ref, xnb_ref, g_ref, w_ref, cw_ref, cb_ref,
              wa_ref, ba_ref, wx_ref, bx_ref, lam_ref, hf_ref, hb_ref,
              xb_scr, af_scr, bf_scr, ab_scr, bb_scr, cf_scr, cbk_scr):
    i = pl.program_id(1)
    n_tiles = pl.num_programs(1)
    tm = xmf_ref.shape[1]

    @pl.when(i == 0)
    def _():
        cf_scr[...] = jnp.zeros(cf_scr.shape, F32)
        cbk_scr[...] = jnp.zeros(cbk_scr.shape, F32)

    shared = (g_ref, w_ref, cw_ref, cb_ref, wa_ref, ba_ref, wx_ref, bx_ref, lam_ref, xb_scr)
    _lru_gates(xmf_ref, xpf_ref, xnf_ref, *shared, af_scr, bf_scr, i, n_tiles, 0)
    _lru_gates(xmb_ref, xpb_ref, xnb_ref, *shared, ab_scr, bb_scr, n_tiles - 1 - i, n_tiles, 1)

    row = lax.broadcasted_iota(jnp.int32, (HALO, LRU_WIDTH), 0)
    nblk = tm // HALO

    def block(j, carry):
        cf, cbk = carry
        jf = pl.multiple_of(j * HALO, HALO)
        jb = pl.multiple_of((nblk - 1 - j) * HALO, HALO)
        a, bv = af_scr[pl.ds(jf, HALO), :], bf_scr[pl.ds(jf, HALO), :]
        a2, bv2 = ab_scr[pl.ds(jb, HALO), :], bb_scr[pl.ds(jb, HALO), :]
        for sh in (1, 2, 4):
            keep = row >= sh
            a_s = jnp.where(keep, pltpu.roll(a, sh, 0), 1.0)
            b_s = jnp.where(keep, pltpu.roll(bv, sh, 0), 0.0)
            bv = a * b_s + bv
            a = a * a_s
            keep2 = row < HALO - sh
            a2_s = jnp.where(keep2, pltpu.roll(a2, HALO - sh, 0), 1.0)
            b2_s = jnp.where(keep2, pltpu.roll(bv2, HALO - sh, 0), 0.0)
            bv2 = a2 * b2_s + bv2
            a2 = a2 * a2_s
        hf = a * cf + bv
        hb = a2 * cbk + bv2
        hf_ref[0, pl.ds(jf, HALO), :] = hf
        hb_ref[0, pl.ds(jb, HALO), :] = hb
        return hf[HALO - 1:HALO], hb[0:1]

    cf, cbk = lax.fori_loop(0, nblk, block, (cf_scr[...], cbk_scr[...]))
    cf_scr[...] = cf
    cbk_scr[...] = cbk


def _lru(x, g, w_x, cw, cb, wa_bd, ba, wx_bd, bx, lam, *, tm):
    b, s, d = x.shape
    n = s // tm
    nb = tm // HALO
    w = LRU_WIDTH
    fwd = _halo_specs(tm, d, n)
    bwd = [
        pl.BlockSpec((1, tm, d), lambda bi, i: (bi, n - 1 - i, 0)),
        pl.BlockSpec((1, HALO, d), lambda bi, i: (bi, jnp.maximum((n - 1 - i) * nb - 1, 0), 0)),
        pl.BlockSpec((1, HALO, d), lambda bi, i: (bi, jnp.minimum((n - i) * nb, n * nb - 1), 0)),
    ]
    return pl.pallas_call(
        _lru_body,
        grid=(b, n),
        in_specs=fwd + bwd + [
            _const_spec((1, d)),
            _const_spec((d, w)),
            _const_spec(cw.shape),
            _const_spec((1, w)),
            _const_spec((2, w, w)),
            _const_spec((2, w)),
            _const_spec((2, w, w)),
            _const_spec((2, w)),
            _const_spec((2, w)),
        ],
        out_specs=[
            pl.BlockSpec((1, tm, w), lambda bi, i: (bi, i, 0)),
            pl.BlockSpec((1, tm, w), lambda bi, i: (bi, n - 1 - i, 0)),
        ],
        out_shape=[jax.ShapeDtypeStruct((b, s, w), F32)] * 2,
        scratch_shapes=[pltpu.VMEM((tm + 2 * HALO, w), F32)]
        + [pltpu.VMEM((tm, w), F32)] * 4
        + [pltpu.VMEM((1, w), F32)] * 2,
        compiler_params=_params("parallel", "arbitrary"),
        name="rglru",
    )(x, x, x, x, x, x, g.reshape(1, d), w_x.astype(BF16), cw, cb.reshape(1, -1),
      wa_bd.astype(BF16), ba, wx_bd.astype(BF16), bx, lam)


def _block_diag(wblk):
    two, n, c, _ = wblk.shape
    eye = jnp.eye(n, dtype=wblk.dtype)
    return jnp.einsum("dnij,nm->dnimj", wblk, eye).reshape(two, n * c, n * c)


def _gelu_tanh(x):
    return 0.5 * x * (1.0 + jnp.tanh(0.7978845608028654 * (x + 0.044715 * (x * x * x))))


def _merge_body(x_ref, g_ref, oa_ref, ob_ref, oc_ref, hf_ref, hb_ref, wgate_ref, wlg_ref, wbr_ref,
                wout_ref, o_ref):
    x = x_ref[0]
    d = x.shape[1]
    h = _rms_rows(x, g_ref[...]).astype(BF16)
    od = (_gelu_tanh(_dot(h, wlg_ref[...])) * (hf_ref[0] + hb_ref[0])).astype(BF16)
    ys = (
        _dot_tn(oa_ref[0], wbr_ref[0]),
        _dot(ob_ref[0], wbr_ref[1]),
        _dot_tn(oc_ref[0], wbr_ref[2]),
        _dot(od, wbr_ref[3]),
    )
    merged = jnp.zeros(x.shape, F32)
    for n, y in enumerate(ys):
        gate = jax.nn.sigmoid(_dot(h, wgate_ref[:, n * d:(n + 1) * d]))
        merged = merged + gate * y
    o_ref[0] = x + _dot(merged.astype(BF16), wout_ref[...])


def _merge(x, g, oa_t, ob, oc_t, hf, hb, w_gate, w_lg, w_branch, w_out, *, tm):
    b, s, d = x.shape
    w = oa_t.shape[1]
    row = lambda width: pl.BlockSpec((1, tm, width), lambda bi, i: (bi, i, 0))
    col = pl.BlockSpec((1, w, tm), lambda bi, i: (bi, 0, i))
    return pl.pallas_call(
        _merge_body,
        grid=(b, s // tm),
        in_specs=[
            row(d), _const_spec((1, d)), col, row(w), col, row(w), row(w),
            _const_spec((d, N_BRANCH * d)),
            _const_spec((d, w)),
            _const_spec((N_BRANCH, w, d)),
            _const_spec((d, d)),
        ],
        out_specs=row(d),
        out_shape=jax.ShapeDtypeStruct((b, s, d), F32),
        compiler_params=_params("parallel", "parallel"),
        name="merge",
    )(x, g.reshape(1, d), oa_t, ob, oc_t, hf, hb, w_gate.astype(BF16), w_lg.astype(BF16),
      w_branch.astype(BF16), w_out.astype(BF16))


def _row_tile(n, target):
    t = min(n, target)
    while n % t:
        t //= 2
    return t


def kernel(x, ffn1_norm, ffn1_wi, ffn1_wo, mix_norm, w_in, gqa_q_norm, gqa_k_norm, sc_conv_w, sc_conv_b, mla_qa_norm, mla_wq_up, mla_kva_norm, mla_wkv_up, mla_q_norm, mla_k_norm, lru_conv_w, lru_conv_b, lru_wa, lru_ba, lru_wx, lru_bx, lru_lambda, w_branch, w_out, ffn2_norm, ffn2_wi, ffn2_wo):
    b, s, d = x.shape
    depth = ffn1_wi.shape[0]
    tm = _row_tile(s, 512)
    tq = _row_tile(s, 2048)
    tk = _row_tile(s, 512)
    strip = min(tq, 256)
    gqa_cos, gqa_sin = _rope_tables(s, GQA_HEAD_DIM)
    mla_cos, mla_sin = _rope_tables(s, MLA_ROPE)

    n_q = GQA_HEADS * GQA_HEAD_DIM
    n_kv = GQA_KV_HEADS * GQA_HEAD_DIM
    o_sc = n_q + 2 * n_kv
    o_qlat = o_sc + 3 * SC_WIDTH
    o_kr = o_qlat + MLA_Q_LORA + MLA_KV_LORA
    o_lru = o_kr + MLA_ROPE
    o_gate = o_lru + 2 * LRU_WIDTH

    for l in range(depth):
        x = _ffn(x.reshape(b * s, d), ffn1_norm[l], ffn1_wi[l], ffn1_wo[l], tm=tm).reshape(b, s, d)

        wl = w_in[l]
        g = mix_norm[l]
        qt, kk, vt = _gqa_prep(x, g, wl[:, :o_sc], gqa_q_norm[l], gqa_k_norm[l], gqa_cos, gqa_sin, tm=tk)
        oa_t = _attention(qt, kk, vt, _logit_bound(gqa_q_norm[l], gqa_k_norm[l], GQA_HEAD_DIM),
                          tq=tq, strip=strip).reshape(b, GQA_HEADS * GQA_HEAD_DIM, s)
        qt, kk, vt = _mla_prep(x, g, wl[:, o_qlat:o_kr], wl[:, o_kr:o_lru], mla_qa_norm[l],
                               mla_kva_norm[l], mla_wq_up[l], mla_wkv_up[l], mla_q_norm[l],
                               mla_k_norm[l], mla_cos, mla_sin, tm=tk)
        oc_t = _attention(qt, kk, vt, _logit_bound(mla_q_norm[l], mla_k_norm[l], MLA_NOPE + MLA_ROPE),
                          tq=tq, strip=strip).reshape(b, MLA_HEADS * MLA_V, s)
        ob = _shortconv(x, g, wl[:, o_sc:o_qlat], sc_conv_w[l], sc_conv_b[l], tm=tm)
        hf, hb = _lru(x, g, wl[:, o_lru + LRU_WIDTH:o_gate], lru_conv_w[l], lru_conv_b[l],
                      _block_diag(lru_wa[l]), lru_ba[l], _block_diag(lru_wx[l]), lru_bx[l],
                      lru_lambda[l], tm=tm)
        x = _merge(x, g, oa_t, ob, oc_t, hf, hb, wl[:, o_gate:], wl[:, o_lru:o_lru + LRU_WIDTH],
                   w_branch[l], w_out[l], tm=tm)

        x = _ffn(x.reshape(b * s, d), ffn2_norm[l], ffn2_wi[l], ffn2_wo[l], tm=tm).reshape(b, s, d)
    return x
```
